```python
import math
import jax, jax.numpy as jnp
from jax import lax
import numpy as np

D_MODEL = 1024
BATCH = 16
SEQ = 4096
DEPTH = 1
DEC_BATCH = 128
DEC_SEQ = 1
PAST_LEN = 8192
PAGE_SIZE = 128

POOL_WIDTH = D_MODEL // 2
POOL_WINDOWS = (2, 4, 8, 16)
POOL_GROUPS = 4
POOL_GROUP_DIM = POOL_WIDTH // POOL_GROUPS
POOL_BUF = 15
N_HEADS = 8
N_KV_HEADS = 4
HEAD_DIM = 64
GROUP = N_HEADS // N_KV_HEADS
ATTN_WIDTH = N_HEADS * HEAD_DIM
KV_WIDTH = N_KV_HEADS * HEAD_DIM
IDX_HEADS = 4
IDX_DIM = 64
TOPK_MAX = 256
Q_BLOCK = 128
N_BUCKETS = 32
MAX_DISTANCE = 128
D_FF = -(-(8 * D_MODEL) // (3 * 256)) * 256
EPS = 1e-6
IN_SPLITS = (POOL_WIDTH, ATTN_WIDTH, KV_WIDTH, KV_WIDTH, IDX_HEADS * IDX_DIM, IDX_DIM, IDX_HEADS, 2 * D_MODEL)
IN_WIDTH = POOL_WIDTH + ATTN_WIDTH + 2 * KV_WIDTH + IDX_HEADS * IDX_DIM + IDX_DIM + IDX_HEADS + 2 * D_MODEL

kernel_name = 'hybrid_pool_dsa_decoder_step'


def rms_norm(x, g):
    xf = x.astype(jnp.float32)
    xf = xf * lax.rsqrt(jnp.mean(xf * xf, axis=-1, keepdims=True) + EPS)
    return xf.astype(x.dtype) * g


def project_in(x, ln1, w_in):
    B, S = x.shape[:2]
    z = jnp.einsum('bsd,de->bse', rms_norm(x, ln1), w_in)
    offs = np.cumsum((0,) + IN_SPLITS)
    u, q, k, v, qi, ki, wi, gates = [z[..., offs[i]:offs[i + 1]] for i in range(len(IN_SPLITS))]
    q = q.reshape(B, S, N_HEADS, HEAD_DIM)
    k = k.reshape(B, S, N_KV_HEADS, HEAD_DIM)
    v = v.reshape(B, S, N_KV_HEADS, HEAD_DIM)
    qi = qi.reshape(B, S, IDX_HEADS, IDX_DIM)
    return u, q, k, v, qi, ki, wi, gates


def pool_mixer(u_ext, n_prev, pos0, pool_w, pool_scale):
    T = u_ext.shape[1] - n_prev
    pos = pos0 + jnp.arange(T, dtype=jnp.int32)
    outs = []
    for gi, w in enumerate(POOL_WINDOWS):
        ug = u_ext[..., gi * POOL_GROUP_DIM:(gi + 1) * POOL_GROUP_DIM].astype(jnp.float32)
        cs = jnp.cumsum(ug, axis=1)
        lagged = jnp.pad(cs, ((0, 0), (w, 0), (0, 0)))[:, :cs.shape[1]]
        win = (cs - lagged)[:, n_prev:]
        count = jnp.minimum(w, pos + 1).astype(jnp.float32)
        d = win / count[None, :, None] - ug[:, n_prev:]
        outs.append(jnp.einsum('btc,ce->bte', d.astype(u_ext.dtype), pool_w[gi]))
    return jnp.concatenate(outs, axis=-1) * pool_scale


def rel_bucket(dist):
    max_exact = N_BUCKETS // 2
    d = jnp.maximum(dist, 1).astype(jnp.float32)
    large = max_exact + (jnp.log(d / max_exact) / math.log(MAX_DISTANCE / max_exact)
                         * (N_BUCKETS - max_exact)).astype(jnp.int32)
    return jnp.where(dist < max_exact, dist, jnp.minimum(large, N_BUCKETS - 1))


def indexer_select(qi, wi, ki, q_pos, k_pos, n_top):
    s = jnp.einsum('bqhd,bld->bqhl', qi.astype(jnp.float32), ki.astype(jnp.float32)) * IDX_DIM ** -0.5
    w = wi.astype(jnp.float32) * IDX_HEADS ** -0.5
    score = jnp.einsum('bqh,bqhl->bql', w, jax.nn.relu(s))
    score = jnp.where((k_pos[None, :] <= q_pos[:, None])[None], score, -jnp.inf)
    return lax.top_k(score, n_top)[1]


def attend(q, k_sel, v_sel, sel_pos, q_pos, rel_bias):
    B, Q = q.shape[:2]
    K = sel_pos.shape[-1]
    qg = q.reshape(B, Q, N_KV_HEADS, GROUP, HEAD_DIM)
    logits = jnp.einsum('bqcgd,bqkcd->bqcgk', qg, k_sel).astype(jnp.float32) * HEAD_DIM ** -0.5
    dist = q_pos[None, :, None] - sel_pos
    bias = rel_bias[rel_bucket(jnp.maximum(dist, 0))].astype(jnp.float32)
    bias = jnp.moveaxis(bias.reshape(B, Q, K, N_KV_HEADS, GROUP), 2, -1)
    valid = (dist >= 0)[:, :, None, None, :]
    probs = jax.nn.softmax(jnp.where(valid, logits + bias, -jnp.inf), axis=-1)
    out = jnp.einsum('bqcgk,bqkcd->bqcgd', probs.astype(v_sel.dtype), v_sel)
    return out.reshape(B, Q, ATTN_WIDTH)


def prompt_sparse_attention(q, k, v, qi, ki, wi, rel_bias):
    B, S = q.shape[:2]
    n_top = min(TOPK_MAX, S // 4)
    n_blk = S // Q_BLOCK
    k_pos = jnp.arange(S, dtype=jnp.int32)

    def to_blocks(a):
        return jnp.moveaxis(a.reshape((B, n_blk, Q_BLOCK) + a.shape[2:]), 1, 0)

    def one_block(args):
        qb, qib, wib, start = args
        q_pos = start + jnp.arange(Q_BLOCK, dtype=jnp.int32)
        idx = indexer_select(qib, wib, ki, q_pos, k_pos, n_top)
        k_sel = jax.vmap(lambda kb, ib: kb[ib])(k, idx)
        v_sel = jax.vmap(lambda vb, ib: vb[ib])(v, idx)
        return attend(qb, k_sel, v_sel, idx, q_pos, rel_bias)

    starts = jnp.arange(n_blk, dtype=jnp.int32) * Q_BLOCK
    out = lax.map(one_block, (to_blocks(q), to_blocks(qi), to_blocks(wi), starts))
    return jnp.moveaxis(out, 0, 1).reshape(B, S, ATTN_WIDTH)


def sample_sparse_attention(q, k_new, v_new, qi, ki_new, wi, cache_k, cache_v, cache_idx_k, page_table, rel_bias):
    B, T = q.shape[:2]
    n_past = page_table.shape[1] * PAGE_SIZE
    L = n_past + T
    n_top = min(TOPK_MAX, L // 4)
    ki_past = cache_idx_k[page_table].reshape(B, n_past, IDX_DIM)
    ki_all = jnp.concatenate([ki_past.astype(ki_new.dtype), ki_new], axis=1)
    q_pos = n_past + jnp.arange(T, dtype=jnp.int32)
    k_pos = jnp.arange(L, dtype=jnp.int32)
    idx = indexer_select(qi, wi, ki_all, q_pos, k_pos, n_top)
    in_past = (idx < n_past)[..., None, None]
    pidx = jnp.minimum(idx, n_past - 1)
    phys = jnp.take_along_axis(page_table, (pidx // PAGE_SIZE).reshape(B, -1), axis=1).reshape(pidx.shape)
    off = pidx % PAGE_SIZE
    nidx = jnp.clip(idx - n_past, 0, T - 1)
    k_sel = jnp.where(in_past, cache_k[phys, off].astype(k_new.dtype), jax.vmap(lambda kb, ib: kb[ib])(k_new, nidx))
    v_sel = jnp.where(in_past, cache_v[phys, off].astype(v_new.dtype), jax.vmap(lambda vb, ib: vb[ib])(v_new, nidx))
    return attend(q, k_sel, v_sel, idx, q_pos, rel_bias)


def merge_ffn(x, a_pool, a_attn, gates, w_pool_proj, w_attn_proj, w_out, ln2, w_gate_up, w_down):
    g = jax.nn.sigmoid(gates.astype(jnp.float32)).astype(x.dtype)
    m = (g[..., :D_MODEL] * jnp.einsum('bsc,cd->bsd', a_pool, w_pool_proj)
         + g[..., D_MODEL:] * jnp.einsum('bsc,cd->bsd', a_attn, w_attn_proj))
    h = x + jnp.einsum('bsd,de->bse', m, w_out)
    f = jnp.einsum('bsd,df->bsf', rms_norm(h, ln2), w_gate_up)
    return h + jnp.einsum('bsf,fd->bsd', jax.nn.silu(f[..., :D_FF]) * f[..., D_FF:], w_down)


def setup_inputs(seed: int = 0) -> dict:
    key = jax.random.key(seed)
    ks = jax.random.split(key, 20)
    n_pages = PAST_LEN // PAGE_SIZE
    n_phys = (DEC_BATCH * n_pages * 5) // 4

    def nrm(k, shape, scale):
        return jax.random.normal(k, shape, jnp.float32) * scale

    perm = jax.random.permutation(ks[0], n_phys)
    page_table = perm[:DEC_BATCH * n_pages].reshape(DEC_BATCH, n_pages).astype(jnp.int32)
    return {
        'x_prompt': nrm(ks[1], (BATCH, SEQ, D_MODEL), 1.0),
        'x_sample': nrm(ks[2], (DEC_BATCH, DEC_SEQ, D_MODEL), 1.0),
        'cache_k': nrm(ks[3], (DEPTH, n_phys, PAGE_SIZE, N_KV_HEADS, HEAD_DIM), 1.0),
        'cache_v': nrm(ks[4], (DEPTH, n_phys, PAGE_SIZE, N_KV_HEADS, HEAD_DIM), 1.0),
        'cache_idx_k': nrm(ks[5], (DEPTH, n_phys, PAGE_SIZE, IDX_DIM), 1.0),
        'state_pool': nrm(ks[6], (DEPTH, DEC_BATCH, POOL_BUF, POOL_WIDTH), 1.0),
        'page_table': page_table,
        'ln1': 1.0 + nrm(ks[7], (DEPTH, D_MODEL), 0.05),
        'w_in': nrm(ks[8], (DEPTH, D_MODEL, IN_WIDTH), D_MODEL ** -0.5),
        'pool_w': nrm(ks[9], (DEPTH, POOL_GROUPS, POOL_GROUP_DIM, POOL_GROUP_DIM), POOL_GROUP_DIM ** -0.5),
        'pool_scale': 1.0 + nrm(ks[10], (DEPTH, POOL_WIDTH), 0.1),
        'w_pool_proj': nrm(ks[11], (DEPTH, POOL_WIDTH, D_MODEL), POOL_WIDTH ** -0.5),
        'w_attn_proj': nrm(ks[12], (DEPTH, ATTN_WIDTH, D_MODEL), ATTN_WIDTH ** -0.5),
        'w_out': nrm(ks[13], (DEPTH, D_MODEL, D_MODEL), D_MODEL ** -0.5),
        'ln2': 1.0 + nrm(ks[14], (DEPTH, D_MODEL), 0.05),
        'w_gate_up': nrm(ks[15], (DEPTH, D_MODEL, 2 * D_FF), D_MODEL ** -0.5),
        'w_down': nrm(ks[16], (DEPTH, D_FF, D_MODEL), D_FF ** -0.5),
        'rel_bias': nrm(ks[17], (N_BUCKETS, N_HEADS), 0.5),
        'ln_final': 1.0 + nrm(ks[18], (D_MODEL,), 0.05),
    }


def reference(x_prompt, x_sample, cache_k, cache_v, cache_idx_k, state_pool, page_table, ln1, w_in, pool_w,
              pool_scale, w_pool_proj, w_attn_proj, w_out, ln2, w_gate_up, w_down, rel_bias, ln_final):
    n_past = page_table.shape[1] * PAGE_SIZE
    hp, hs = x_prompt, x_sample
    kp, vp, kip, pp, ksm, vsm, kism, psm = [], [], [], [], [], [], [], []
    for l in range(DEPTH):
        u, q, k, v, qi, ki, wi, gates = project_in(hp, ln1[l], w_in[l])
        a_pool = pool_mixer(u, 0, 0, pool_w[l], pool_scale[l])
        a_attn = prompt_sparse_attention(q, k, v, qi, ki, wi, rel_bias)
        kp.append(k); vp.append(v); kip.append(ki); pp.append(u[:, -POOL_BUF:])
        hp = merge_ffn(hp, a_pool, a_attn, gates, w_pool_proj[l], w_attn_proj[l], w_out[l], ln2[l], w_gate_up[l], w_down[l])
        u, q, k, v, qi, ki, wi, gates = project_in(hs, ln1[l], w_in[l])
        u_ext = jnp.concatenate([state_pool[l].astype(u.dtype), u], axis=1)
        a_pool = pool_mixer(u_ext, POOL_BUF, n_past, pool_w[l], pool_scale[l])
        a_attn = sample_sparse_attention(q, k, v, qi, ki, wi, cache_k[l], cache_v[l], cache_idx_k[l], page_table, rel_bias)
        ksm.append(k); vsm.append(v); kism.append(ki); psm.append(u_ext[:, -POOL_BUF:])
        hs = merge_ffn(hs, a_pool, a_attn, gates, w_pool_proj[l], w_attn_proj[l], w_out[l], ln2[l], w_gate_up[l], w_down[l])
    y_prompt = rms_norm(hp, ln_final)
    y_sample = rms_norm(hs, ln_final)
    new_k_prompt = jnp.stack(kp)
    new_v_prompt = jnp.stack(vp)
    new_idx_k_prompt = jnp.stack(kip)
    new_pool_prompt = jnp.stack(pp)
    new_k_sample = jnp.stack(ksm)
    new_v_sample = jnp.stack(vsm)
    new_idx_k_sample = jnp.stack(kism)
    new_pool_sample = jnp.stack(psm)
    return (y_prompt, y_sample, new_k_prompt, new_v_prompt, new_idx_k_prompt, new_pool_prompt,
            new_k_sample, new_v_sample, new_idx_k_sample, new_pool_sample)
```

```python
import functools
import math

import jax
import jax.numpy as jnp
import numpy as np
from jax import lax
from jax.experimental import pallas as pl
from jax.experimental.pallas import tpu as pltpu

D_MODEL = 1024
PAGE_SIZE = 128
POOL_WIDTH = D_MODEL // 2
POOL_WINDOWS = (2, 4, 8, 16)
POOL_GROUP_DIM = POOL_WIDTH // len(POOL_WINDOWS)
POOL_BUF = 15
N_HEADS = 8
N_KV_HEADS = 4
HEAD_DIM = 64
GROUP = N_HEADS // N_KV_HEADS
ATTN_WIDTH = N_HEADS * HEAD_DIM
KV_WIDTH = N_KV_HEADS * HEAD_DIM
IDX_HEADS = 4
IDX_DIM = 64
TOPK_MAX = 256
N_BUCKETS = 32
MAX_DISTANCE = 128
D_FF = -(-(8 * D_MODEL) // (3 * 256)) * 256
EPS = 1e-6
IN_SPLITS = (POOL_WIDTH, ATTN_WIDTH, KV_WIDTH, KV_WIDTH, IDX_HEADS * IDX_DIM, IDX_DIM, IDX_HEADS, 2 * D_MODEL)

LANES = 128
QBLK = 128
SCORE_CHUNK = 512
INT_MIN = -(2 ** 31)
NEG = -1e30
VMEM_LIMIT = 56 * 1024 * 1024

C_U, C_Q, C_K, C_V, C_QQ, C_KK, C_WI, C_G = 0, 512, 1024, 1280, 1536, 2048, 2176, 2304
C_END = C_G + 2 * D_MODEL

F32 = jnp.float32
BF16 = jnp.bfloat16


def _dot(a, b):
    return jnp.dot(a, b, preferred_element_type=F32)


def _dot_nt(a, b, precision=None):
    return lax.dot_general(a, b, (((1,), (1,)), ((), ())), preferred_element_type=F32, precision=precision)


def _rms(x, g):
    return (x * lax.rsqrt(jnp.mean(x * x, axis=-1, keepdims=True) + EPS)) * g


def _hi_lo(z):
    hi = z.astype(BF16).astype(F32)
    return hi, z - hi


def _sortable_key(score):
    bits = lax.bitcast_convert_type(score, jnp.int32)
    return jnp.where(bits < 0, -(bits & 0x7FFFFFFF), bits)


def _gated_pool(d_groups, gates, pw_ref, ps_ref, wpp_ref):
    a = [_dot(d.astype(BF16), pw_ref[g]) for g, d in enumerate(d_groups)]
    a = jnp.concatenate(a, axis=1) * ps_ref[...]
    ap = _dot(a.astype(BF16), wpp_ref[...])
    return jax.nn.sigmoid(gates[:, :D_MODEL]) * ap


def _proj_kernel(x_ref, ln1_ref, w_ref, pw_ref, ps_ref, wpp_ref,
                 q_ref, k_ref, v_ref, kb_ref, vb_ref, ki_ref, kis_ref, qis_ref, wi_ref, mp_ref, ga_ref, ul_ref,
                 uext_ref, *, tm, tiles_per_seq):
    i = pl.program_id(0)
    ti = i % tiles_per_seq
    hb = _rms(x_ref[...], ln1_ref[...]).astype(BF16)

    def mm(lo, hi):
        return _dot(hb, w_ref[:, lo:hi])

    q_ref[...] = (mm(C_Q, C_K) * HEAD_DIM ** -0.5).astype(BF16)
    k = mm(C_K, C_V)
    k_ref[...] = k
    kb_ref[...] = k.astype(BF16)
    v = mm(C_V, C_QQ)
    v_ref[...] = v
    vb_ref[...] = v.astype(BF16)

    hi, lo = _hi_lo(mm(C_QQ, C_KK))
    lane = lax.broadcasted_iota(jnp.int32, (1, 4 * LANES), 1) % LANES
    qis_ref[...] = jnp.where(lane < IDX_DIM, hi, lo).astype(BF16)
    zkk = mm(C_KK, C_WI)
    ki_ref[...] = zkk[:, :IDX_DIM]
    hi, lo = _hi_lo(zkk)
    kis_ref[...] = jnp.concatenate([hi, lo], axis=1).astype(BF16)
    wi_ref[...] = mm(C_WI, C_G) * IDX_HEADS ** -0.5

    u = mm(C_U, C_Q)

    @pl.when(ti == 0)
    def _():
        uext_ref[0:16, :] = jnp.zeros((16, POOL_WIDTH), F32)

    @pl.when(ti != 0)
    def _():
        uext_ref[0:16, :] = uext_ref[tm:tm + 16, :]

    uext_ref[16:16 + tm, :] = u
    ul_ref[0] = uext_ref[tm:tm + 16, :]
    pos1 = ti * tm + lax.broadcasted_iota(jnp.int32, (tm, 1), 0) + 1
    ds = []
    for g, w in enumerate(POOL_WINDOWS):
        sl = slice(g * POOL_GROUP_DIM, (g + 1) * POOL_GROUP_DIM)
        win = uext_ref[16:16 + tm, sl]
        for d in range(1, w):
            win = win + uext_ref[16 - d:16 - d + tm, sl]
        cnt = jnp.minimum(w, pos1).astype(F32)
        ds.append(win / cnt - u[:, sl])
    gates = mm(C_G, C_END)
    mp_ref[...] = _gated_pool(ds, gates, pw_ref, ps_ref, wpp_ref)
    ga_ref[...] = jax.nn.sigmoid(gates[:, D_MODEL:])


def _sproj_kernel(x_ref, ln1_ref, w_ref, pw_ref, ps_ref, wpp_ref, st_ref,
                  q_ref, k_ref, v_ref, ki_ref, qi_ref, wi_ref, mp_ref, ga_ref, u_ref, *, n_past):
    hb = _rms(x_ref[...], ln1_ref[...]).astype(BF16)

    def mm(lo, hi):
        return _dot(hb, w_ref[:, lo:hi])

    q_ref[...] = (mm(C_Q, C_K) * HEAD_DIM ** -0.5).astype(BF16)
    k_ref[...] = mm(C_K, C_V)
    v_ref[...] = mm(C_V, C_QQ)
    zqq = mm(C_QQ, C_KK)
    qi_ref[...] = jnp.concatenate([zqq[:, h * LANES:h * LANES + IDX_DIM] for h in range(IDX_HEADS)], axis=1)
    ki_ref[...] = mm(C_KK, C_WI)[:, :IDX_DIM]
    wi_ref[...] = mm(C_WI, C_G) * IDX_HEADS ** -0.5
    u = mm(C_U, C_Q)
    u_ref[...] = u
    ds = []
    for g, w in enumerate(POOL_WINDOWS):
        sl = slice(g * POOL_GROUP_DIM, (g + 1) * POOL_GROUP_DIM)
        win = u[:, sl]
        for d in range(1, w):
            win = win + st_ref[POOL_BUF - d][:, sl]
        ds.append(win / float(min(w, n_past + 1)) - u[:, sl])
    gates = mm(C_G, C_END)
    mp_ref[...] = _gated_pool(ds, gates, pw_ref, ps_ref, wpp_ref)
    ga_ref[...] = jax.nn.sigmoid(gates[:, D_MODEL:])


def _bias_kernel(rb_ref, o_ref):
    i = lax.broadcasted_iota(jnp.int32, (QBLK, LANES), 0)
    j = lax.broadcasted_iota(jnp.int32, (QBLK, LANES), 1)
    max_exact = N_BUCKETS // 2
    for t in range(2):
        dist = jnp.maximum(t * LANES + i - j, 0)
        d = jnp.maximum(dist, 1).astype(F32)
        large = max_exact + (jnp.log(d / max_exact) / math.log(MAX_DISTANCE / max_exact)
                             * (N_BUCKETS - max_exact)).astype(jnp.int32)
        bucket = jnp.where(dist < max_exact, dist, jnp.minimum(large, N_BUCKETS - 1))
        for h in range(N_HEADS):
            acc = jnp.zeros((QBLK, LANES), F32)
            for b in range(N_BUCKETS):
                acc = jnp.where(bucket == b, rb_ref[b, h], acc)
            o_ref[t, h] = acc
    for h in range(N_HEADS):
        o_ref[2, h] = jnp.full((QBLK, LANES), rb_ref[N_BUCKETS - 1, h], F32)


def _topk_mask(keys_ref, madd_ref, nc, cw, k_top):
    rows = keys_ref.shape[0]
    sub = cw // LANES

    def count_ge(thr):
        thr_b = jnp.broadcast_to(thr, (rows, LANES))

        def body(c, acc):
            off = pl.multiple_of(c * cw, LANES)
            for s in range(sub):
                kk = keys_ref[:, pl.ds(off + s * LANES, LANES)]
                acc = acc + jnp.where(kk >= thr_b, 1, 0)
            return acc

        acc = lax.fori_loop(0, nc, body, jnp.zeros((rows, LANES), jnp.int32))
        return jnp.sum(acc, axis=1, keepdims=True)

    def bit_step(it, thr):
        cand = thr + lax.shift_left(jnp.int32(1), 31 - it)
        return jnp.where(count_ge(cand) >= k_top, cand, thr)

    thr = lax.fori_loop(0, 32, bit_step, jnp.full((rows, 1), INT_MIN, jnp.int32))
    thr_b = jnp.broadcast_to(jnp.maximum(thr, INT_MIN + 1), (rows, LANES))

    def mask_body(c, acc):
        off = pl.multiple_of(c * cw, LANES)
        for s in range(sub):
            kk = keys_ref[:, pl.ds(off + s * LANES, LANES)]
            sel = kk >= thr_b
            madd_ref[:, pl.ds(off + s * LANES, LANES)] = jnp.where(sel, 0.0, NEG)
            acc = acc + jnp.where(sel, 1, 0)
        return acc

    acc = lax.fori_loop(0, nc, mask_body, jnp.zeros((rows, LANES), jnp.int32))
    n_sel = jnp.sum(acc, axis=1, keepdims=True)

    @pl.when(jnp.max(n_sel) > k_top)
    def _():
        room = (k_top - count_ge(thr + 1)).astype(F32)
        a = lax.broadcasted_iota(jnp.int32, (LANES, LANES), 0)
        b = lax.broadcasted_iota(jnp.int32, (LANES, LANES), 1)
        upper = jnp.where(a <= b, 1.0, 0.0).astype(BF16)
        eq_thr = jnp.broadcast_to(thr, (rows, LANES))

        def tie_body(c, seen):
            off = pl.multiple_of(c * LANES, LANES)
            kk = keys_ref[:, pl.ds(off, LANES)]
            eq = kk == eq_thr
            ones = jnp.where(eq, 1.0, 0.0)
            rank = _dot(ones.astype(BF16), upper) + seen
            tied = jnp.where(rank <= room, 0.0, NEG)
            tied = jnp.where(kk > INT_MIN, tied, NEG)
            madd_ref[:, pl.ds(off, LANES)] = jnp.where(eq, tied, jnp.where(kk > eq_thr, 0.0, NEG))
            return seen + jnp.sum(ones, axis=1, keepdims=True)

        lax.fori_loop(0, nc * sub, tie_body, jnp.zeros((rows, 1), F32))


def _attn_kernel(q_ref, qis_ref, wi_ref, kis_ref, kb_ref, vb_ref, bt_ref, o_ref,
                 keys_ref, madd_ref, m_ref, l_ref, acc_ref, *, k_top):
    j = pl.program_id(1)
    t0 = j * QBLK
    nc = (t0 + QBLK + SCORE_CHUNK - 1) // SCORE_CHUNK
    row_pos = t0 + lax.broadcasted_iota(jnp.int32, (QBLK, 1), 0)

    wi = wi_ref[...]
    qs = [jnp.concatenate([qis_ref[:, h * LANES:(h + 1) * LANES]] * 2, axis=1) for h in range(IDX_HEADS)]

    def score_body(c, carry):
        off = pl.multiple_of(c * SCORE_CHUNK, SCORE_CHUNK)
        kc = kis_ref[pl.ds(off, SCORE_CHUNK), :]
        sc = jnp.zeros((QBLK, SCORE_CHUNK), F32)
        for h in range(IDX_HEADS):
            s = _dot_nt(qs[h], kc) * IDX_DIM ** -0.5
            sc = sc + wi[:, h:h + 1] * jnp.maximum(s, 0.0)
        kpos = off + lax.broadcasted_iota(jnp.int32, (1, SCORE_CHUNK), 1)
        keys_ref[:, pl.ds(off, SCORE_CHUNK)] = jnp.where(kpos <= row_pos, _sortable_key(sc), INT_MIN)
        return carry

    lax.fori_loop(0, nc, score_body, 0)
    _topk_mask(keys_ref, madd_ref, nc, SCORE_CHUNK, k_top)

    q = q_ref[...]
    q2 = [jnp.concatenate([q[:, (GROUP * g + r) * HEAD_DIM:(GROUP * g + r + 1) * HEAD_DIM] for r in range(GROUP)], axis=0)
          for g in range(N_KV_HEADS)]
    m_ref[...] = jnp.full(m_ref.shape, NEG, F32)
    l_ref[...] = jnp.zeros(l_ref.shape, F32)
    acc_ref[...] = jnp.zeros(acc_ref.shape, F32)

    def att_body(c, carry):
        off = pl.multiple_of(c * LANES, LANES)
        tile = jnp.minimum(j - c, 2)
        madd = madd_ref[:, pl.ds(off, LANES)]
        for g in range(N_KV_HEADS):
            kc = kb_ref[pl.ds(off, LANES), g * HEAD_DIM:(g + 1) * HEAD_DIM]
            vc = vb_ref[pl.ds(off, LANES), g * HEAD_DIM:(g + 1) * HEAD_DIM]
            bias = jnp.concatenate([bt_ref[tile, GROUP * g + r] + madd for r in range(GROUP)], axis=0)
            s = _dot_nt(q2[g], kc) + bias
            m_prev = m_ref[g]
            m_next = jnp.maximum(m_prev, jnp.max(s, axis=1, keepdims=True))
            alpha = jnp.exp(m_prev - m_next)
            p = jnp.exp(s - m_next)
            l_ref[g] = alpha * l_ref[g] + jnp.sum(p, axis=1, keepdims=True)
            acc_ref[g] = alpha * acc_ref[g] + _dot(p.astype(BF16), vc)
            m_ref[g] = m_next
        return carry

    lax.fori_loop(0, j + 1, att_body, 0)
    for g in range(N_KV_HEADS):
        out = acc_ref[g] / l_ref[g]
        for r in range(GROUP):
            h = GROUP * g + r
            o_ref[:, h * HEAD_DIM:(h + 1) * HEAD_DIM] = out[r * QBLK:(r + 1) * QBLK].astype(o_ref.dtype)


def _ffn_kernel(x_ref, mp_ref, ga_ref, aa_ref, wap_ref, wout_ref, ln2_ref, wgu_ref, wd_ref, lnf_ref, y_ref, *, fc):
    att = _dot(aa_ref[...], wap_ref[...])
    m = mp_ref[...] + ga_ref[...] * att
    h = x_ref[...] + _dot(m.astype(BF16), wout_ref[...])
    hn = _rms(h, ln2_ref[...]).astype(BF16)
    y = h
    for c in range(D_FF // fc):
        gate = _dot(hn, wgu_ref[:, c * fc:(c + 1) * fc])
        up = _dot(hn, wgu_ref[:, D_FF + c * fc:D_FF + (c + 1) * fc])
        act = (gate * jax.nn.sigmoid(gate)) * up
        y = y + _dot(act.astype(BF16), wd_ref[c * fc:(c + 1) * fc, :])
    y_ref[...] = _rms(y, lnf_ref[...])


def _sscore_kernel(pt_ref, qi_ref, wi_ref, *refs, pages):
    page_refs, o_ref = refs[:pages], refs[pages]
    kc = jnp.concatenate([r[0] for r in page_refs], axis=0)
    s = _dot_nt(qi_ref[0], kc, precision=lax.Precision.HIGHEST) * IDX_DIM ** -0.5
    o_ref[0] = jnp.sum(wi_ref[0] * jnp.maximum(s, 0.0), axis=0, keepdims=True)


def _sselect_kernel(sc_ref, qi_ref, kin_ref, wi_ref, madd_ref, keys_ref, *, n_past, cw, k_top):
    rows = sc_ref.shape[0]
    keys_ref[:, 0:n_past] = _sortable_key(sc_ref[...])
    kin = kin_ref[...]
    wi = wi_ref[...]
    s_new = jnp.zeros((rows, 1), F32)
    for h in range(IDX_HEADS):
        s = jnp.sum(qi_ref[:, h * IDX_DIM:(h + 1) * IDX_DIM] * kin, axis=1, keepdims=True) * IDX_DIM ** -0.5
        s_new = s_new + wi[:, h:h + 1] * jnp.maximum(s, 0.0)
    lane = lax.broadcasted_iota(jnp.int32, (rows, keys_ref.shape[1] - n_past), 1)
    keys_ref[:, n_past:] = jnp.where(lane == 0, _sortable_key(s_new), INT_MIN)
    _topk_mask(keys_ref, madd_ref, keys_ref.shape[1] // cw, cw, k_top)


def _sattn_kernel(pt_ref, qbd_ref, madd_ref, maddn_ref, sb_ref, sbn_ref, kn_ref, vn_ref, *refs, pages):
    k_refs, v_refs = refs[:pages], refs[pages:2 * pages]
    o_ref, m_ref, l_ref, acc_ref = refs[2 * pages:]
    c = pl.program_id(1)

    @pl.when(c == 0)
    def _():
        m_ref[...] = jnp.full(m_ref.shape, NEG, F32)
        l_ref[...] = jnp.zeros(l_ref.shape, F32)
        acc_ref[...] = jnp.zeros(acc_ref.shape, F32)

    qbd = qbd_ref[0]
    kc = jnp.concatenate([r[0] for r in k_refs], axis=0).astype(BF16)
    vc = jnp.concatenate([r[0] for r in v_refs], axis=0).astype(BF16)
    s = _dot_nt(qbd, kc) + sb_ref[...] + madd_ref[0]
    m_prev = m_ref[...]
    m_next = jnp.maximum(m_prev, jnp.max(s, axis=1, keepdims=True))
    alpha = jnp.exp(m_prev - m_next)
    p = jnp.exp(s - m_next)
    l_ref[...] = alpha * l_ref[...] + jnp.sum(p, axis=1, keepdims=True)
    acc_ref[...] = alpha * acc_ref[...] + _dot(p.astype(BF16), vc)
    m_ref[...] = m_next

    @pl.when(c == pl.num_programs(1) - 1)
    def _():
        s_new = (jnp.sum(qbd.astype(F32) * kn_ref[0], axis=1, keepdims=True)
                 + sbn_ref[:, 0:1] + maddn_ref[0][:, 0:1])
        m_prev = m_ref[...]
        m_next = jnp.maximum(m_prev, s_new)
        alpha = jnp.exp(m_prev - m_next)
        p = jnp.exp(s_new - m_next)
        l = alpha * l_ref[...] + p
        o_ref[0] = (alpha * acc_ref[...] + p * vn_ref[0]) / l


def _full(shape):
    return pl.BlockSpec(shape, lambda *_: (0,) * len(shape))


def _pack_w_in(w):
    offs = np.cumsum((0,) + IN_SPLITS)
    wu, wq, wk, wv, wqi, wki, wwi, wg = [w[:, offs[i]:offs[i + 1]] for i in range(len(IN_SPLITS))]
    wqq = jnp.concatenate([wqi[:, h * IDX_DIM:(h + 1) * IDX_DIM] for h in range(IDX_HEADS) for _ in range(2)], axis=1)
    wkk = jnp.concatenate([wki, wki], axis=1)
    wwp = jnp.pad(wwi, ((0, 0), (0, LANES - IDX_HEADS)))
    return jnp.concatenate([wu, wq, wk, wv, wqq, wkk, wwp, wg], axis=1).astype(BF16)


def _proj_prompt(x2, ln1, w_all, pw, ps, wpp, seq, tm):
    t = x2.shape[0]
    tps = seq // tm
    nb = t // seq
    row = lambda n: pl.BlockSpec((tm, n), lambda i: (i, 0))
    out_shape = (
        jax.ShapeDtypeStruct((t, ATTN_WIDTH), BF16),
        jax.ShapeDtypeStruct((t, KV_WIDTH), F32),
        jax.ShapeDtypeStruct((t, KV_WIDTH), F32),
        jax.ShapeDtypeStruct((t, KV_WIDTH), BF16),
        jax.ShapeDtypeStruct((t, KV_WIDTH), BF16),
        jax.ShapeDtypeStruct((t, IDX_DIM), F32),
        jax.ShapeDtypeStruct((t, 2 * LANES), BF16),
        jax.ShapeDtypeStruct((t, 4 * LANES), BF16),
        jax.ShapeDtypeStruct((t, LANES), F32),
        jax.ShapeDtypeStruct((t, D_MODEL), F32),
        jax.ShapeDtypeStruct((t, D_MODEL), F32),
        jax.ShapeDtypeStruct((nb, 16, POOL_WIDTH), F32),
    )
    out_specs = (row(ATTN_WIDTH), row(KV_WIDTH), row(KV_WIDTH), row(KV_WIDTH), row(KV_WIDTH), row(IDX_DIM),
                 row(2 * LANES), row(4 * LANES), row(LANES), row(D_MODEL), row(D_MODEL),
                 pl.BlockSpec((1, 16, POOL_WIDTH), lambda i: (i // tps, 0, 0)))
    return pl.pallas_call(
        functools.partial(_proj_kernel, tm=tm, tiles_per_seq=tps),
        grid=(t // tm,),
        in_specs=[row(D_MODEL), _full((1, D_MODEL)), _full(w_all.shape), _full(pw.shape), _full((1, POOL_WIDTH)),
                  _full(wpp.shape)],
        out_specs=out_specs,
        out_shape=out_shape,
        scratch_shapes=[pltpu.VMEM((tm + 16, POOL_WIDTH), F32)],
        compiler_params=pltpu.CompilerParams(dimension_semantics=("arbitrary",), vmem_limit_bytes=VMEM_LIMIT),
        name="proj_pool",
    )(x2, ln1, w_all, pw, ps, wpp)


def _proj_sample(x2, ln1, w_all, pw, ps, wpp, st, n_past):
    m = x2.shape[0]
    shapes = [(ATTN_WIDTH, BF16), (KV_WIDTH, F32), (KV_WIDTH, F32), (IDX_DIM, F32), (IDX_HEADS * IDX_DIM, F32),
              (LANES, F32), (D_MODEL, F32), (D_MODEL, F32), (POOL_WIDTH, F32)]
    return pl.pallas_call(
        functools.partial(_sproj_kernel, n_past=n_past),
        out_shape=tuple(jax.ShapeDtypeStruct((m, n), dt) for n, dt in shapes),
        compiler_params=pltpu.CompilerParams(vmem_limit_bytes=VMEM_LIMIT),
        name="proj_pool_sample",
    )(x2, ln1, w_all, pw, ps, wpp, st)


def _bias_tiles(rel_bias):
    return pl.pallas_call(
        _bias_kernel,
        in_specs=[pl.BlockSpec(memory_space=pltpu.SMEM)],
        out_shape=jax.ShapeDtypeStruct((3, N_HEADS, QBLK, LANES), F32),
        name="bias_tiles",
    )(rel_bias)


def _attn_prompt(q, qis, wi, kis, kb, vb, bt, nb, seq, k_top):
    nq = seq // QBLK
    qrow = lambda n: pl.BlockSpec((QBLK, n), lambda b, j: (b * nq + j, 0))
    seqblk = lambda n: pl.BlockSpec((seq, n), lambda b, j: (b, 0))
    return pl.pallas_call(
        functools.partial(_attn_kernel, k_top=k_top),
        grid=(nb, nq),
        in_specs=[qrow(ATTN_WIDTH), qrow(4 * LANES), qrow(LANES), seqblk(2 * LANES), seqblk(KV_WIDTH),
                  seqblk(KV_WIDTH), pl.BlockSpec(bt.shape, lambda b, j: (0, 0, 0, 0))],
        out_specs=qrow(ATTN_WIDTH),
        out_shape=jax.ShapeDtypeStruct((nb * seq, ATTN_WIDTH), BF16),
        scratch_shapes=[pltpu.VMEM((QBLK, seq), jnp.int32), pltpu.VMEM((QBLK, seq), F32),
                        pltpu.VMEM((N_KV_HEADS, GROUP * QBLK, 1), F32), pltpu.VMEM((N_KV_HEADS, GROUP * QBLK, 1), F32),
                        pltpu.VMEM((N_KV_HEADS, GROUP * QBLK, HEAD_DIM), F32)],
        compiler_params=pltpu.CompilerParams(dimension_semantics=("arbitrary", "arbitrary"),
                                             vmem_limit_bytes=VMEM_LIMIT),
        name="sparse_attn",
    )(q, qis, wi, kis, kb, vb, bt)


def _merge_ffn(x2, mp, ga, aa, wap, wout, ln2, wgu, wd, lnf, tm):
    t = x2.shape[0]
    row = lambda n: pl.BlockSpec((tm, n), lambda i: (i, 0))
    return pl.pallas_call(
        functools.partial(_ffn_kernel, fc=256),
        grid=(t // tm,),
        in_specs=[row(D_MODEL), row(D_MODEL), row(D_MODEL), row(ATTN_WIDTH), _full(wap.shape), _full(wout.shape),
                  _full((1, D_MODEL)), _full(wgu.shape), _full(wd.shape), _full((1, D_MODEL))],
        out_specs=row(D_MODEL),
        out_shape=jax.ShapeDtypeStruct((t, D_MODEL), F32),
        compiler_params=pltpu.CompilerParams(dimension_semantics=("arbitrary",), vmem_limit_bytes=VMEM_LIMIT),
        name="merge_ffn",
    )(x2, mp, ga, aa, wap, wout, ln2, wgu, wd, lnf)


def _sample_scores(page_table, qi, wi4, cache_ik, pages):
    bd, n_pages = page_table.shape
    page_spec = lambda i: pl.BlockSpec((1, PAGE_SIZE, IDX_DIM), lambda b, c, pt: (pt[b, c * pages + i], 0, 0))
    grid_spec = pltpu.PrefetchScalarGridSpec(
        num_scalar_prefetch=1,
        grid=(bd, n_pages // pages),
        in_specs=[pl.BlockSpec((1, IDX_HEADS, IDX_DIM), lambda b, c, pt: (b, 0, 0)),
                  pl.BlockSpec((1, IDX_HEADS, 1), lambda b, c, pt: (b, 0, 0))] + [page_spec(i) for i in range(pages)],
        out_specs=pl.BlockSpec((1, 1, pages * PAGE_SIZE), lambda b, c, pt: (b, 0, c)),
    )
    return pl.pallas_call(
        functools.partial(_sscore_kernel, pages=pages),
        grid_spec=grid_spec,
        out_shape=jax.ShapeDtypeStruct((bd, 1, n_pages * PAGE_SIZE), F32),
        compiler_params=pltpu.CompilerParams(dimension_semantics=("arbitrary", "arbitrary")),
        name="sample_scores",
    )(page_table, qi, wi4, *([cache_ik] * pages))


def _sample_select(scores, qi, ki_new, wi, n_past, cw, k_top):
    bd = scores.shape[0]
    width = n_past + LANES
    return pl.pallas_call(
        functools.partial(_sselect_kernel, n_past=n_past, cw=cw, k_top=k_top),
        out_shape=jax.ShapeDtypeStruct((bd, width), F32),
        scratch_shapes=[pltpu.VMEM((bd, width), jnp.int32)],
        compiler_params=pltpu.CompilerParams(vmem_limit_bytes=VMEM_LIMIT),
        name="sample_select",
    )(scores, qi, ki_new, wi)


def _sample_attn(page_table, qbd, madd, madd_new, sbias, sbias_new, k_new, v_new, cache_k, cache_v, pages):
    bd, n_pages = page_table.shape
    chunk = pages * PAGE_SIZE
    page_spec = lambda i: pl.BlockSpec((1, PAGE_SIZE, KV_WIDTH), lambda b, c, pt: (pt[b, c * pages + i], 0, 0))
    per_b = lambda shape: pl.BlockSpec((1,) + shape, lambda b, c, pt: (b, 0, 0))
    grid_spec = pltpu.PrefetchScalarGridSpec(
        num_scalar_prefetch=1,
        grid=(bd, n_pages // pages),
        in_specs=[per_b((N_HEADS, KV_WIDTH)),
                  pl.BlockSpec((1, 1, chunk), lambda b, c, pt: (b, 0, c)),
                  per_b((1, LANES)),
                  pl.BlockSpec((N_HEADS, chunk), lambda b, c, pt: (0, c)),
                  pl.BlockSpec((N_HEADS, LANES), lambda b, c, pt: (0, 0)),
                  per_b((1, KV_WIDTH)), per_b((1, KV_WIDTH))]
                 + [page_spec(i) for i in range(pages)] * 2,
        out_specs=per_b((N_HEADS, KV_WIDTH)),
        scratch_shapes=[pltpu.VMEM((N_HEADS, 1), F32), pltpu.VMEM((N_HEADS, 1), F32),
                        pltpu.VMEM((N_HEADS, KV_WIDTH), F32)],
    )
    return pl.pallas_call(
        functools.partial(_sattn_kernel, pages=pages),
        grid_spec=grid_spec,
        out_shape=jax.ShapeDtypeStruct((bd, N_HEADS, KV_WIDTH), F32),
        compiler_params=pltpu.CompilerParams(dimension_semantics=("arbitrary", "arbitrary"),
                                             vmem_limit_bytes=VMEM_LIMIT),
        name="sample_attn",
    )(page_table, qbd, madd, madd_new, sbias, sbias_new, k_new, v_new, *([cache_k] * pages), *([cache_v] * pages))


def _pick(n, prefs):
    for p in prefs:
        if n % p == 0:
            return p
    return n


def kernel(x_prompt, x_sample, cache_k, cache_v, cache_idx_k, state_pool, page_table, ln1, w_in, pool_w, pool_scale,
           w_pool_proj, w_attn_proj, w_out, ln2, w_gate_up, w_down, rel_bias, ln_final):
    assert w_in.shape[0] == 1, "one layer"
    nb, seq, _ = x_prompt.shape
    bd, dec_seq, _ = x_sample.shape
    assert dec_seq == 1 and seq % SCORE_CHUNK == 0
    n_pages = page_table.shape[1]
    n_past = n_pages * PAGE_SIZE
    n_phys = cache_k.shape[1]

    w_all = _pack_w_in(w_in[0])
    ln1r, ln2r, lnfr = ln1[0][None], ln2[0][None], ln_final[None]
    pw = pool_w[0].astype(BF16)
    ps = pool_scale[0][None]
    wpp, wap, wout = w_pool_proj[0].astype(BF16), w_attn_proj[0].astype(BF16), w_out[0].astype(BF16)
    wgu, wd = w_gate_up[0].astype(BF16), w_down[0].astype(BF16)
    bt = _bias_tiles(rel_bias)

    xp = x_prompt.reshape(nb * seq, D_MODEL)
    q, k, v, kb, vb, ki, kis, qis, wi, mp, ga, ul = _proj_prompt(xp, ln1r, w_all, pw, ps, wpp, seq, _pick(seq, (512, 256, 128)))
    aa = _attn_prompt(q, qis, wi, kis, kb, vb, bt, nb, seq, min(TOPK_MAX, seq // 4))
    y_prompt = _merge_ffn(xp, mp, ga, aa, wap, wout, ln2r, wgu, wd, lnfr, _pick(nb * seq, (256, 128))).reshape(nb, seq, D_MODEL)

    xs = x_sample.reshape(bd, D_MODEL)
    st = jnp.swapaxes(state_pool[0], 0, 1)
    qs, ks, vs, kin, qin, wis, mps, gas, us = _proj_sample(xs, ln1r, w_all, pw, ps, wpp, st, n_past)
    pages = _pick(n_pages, (16, 8, 4, 2))
    scores = _sample_scores(page_table, qin.reshape(bd, IDX_HEADS, IDX_DIM), wis[:, :IDX_HEADS, None],
                            cache_idx_k[0], pages).reshape(bd, n_past)
    width = n_past + LANES
    madd = _sample_select(scores, qin, kin, wis, n_past, _pick(width, (640, 512, 384, 256, 128)),
                          min(TOPK_MAX, (n_past + 1) // 4))
    head_kv = np.arange(N_HEADS) // GROUP
    onehot = jnp.asarray(head_kv[:, None] == np.arange(N_KV_HEADS)[None, :])
    qbd = jnp.where(onehot[None, :, :, None], qs.reshape(bd, N_HEADS, 1, HEAD_DIM), 0).reshape(bd, N_HEADS, KV_WIDTH)
    far = jnp.broadcast_to(bt[2, :, 0, 0:1], (N_HEADS, n_past - LANES))
    sbias = jnp.concatenate([far, bt[1, :, 0, :]], axis=1)
    sbias_new = jnp.broadcast_to(bt[0, :, 0, 0:1], (N_HEADS, LANES))
    ao = _sample_attn(page_table, qbd, madd[:, None, :n_past], madd[:, None, n_past:], sbias, sbias_new,
                      ks[:, None, :], vs[:, None, :], cache_k[0].reshape(n_phys, PAGE_SIZE, KV_WIDTH),
                      cache_v[0].reshape(n_phys, PAGE_SIZE, KV_WIDTH), pages)
    aas = jnp.sum(jnp.where(onehot[None, :, :, None], ao.reshape(bd, N_HEADS, N_KV_HEADS, HEAD_DIM), 0), axis=2)
    aas = aas.reshape(bd, ATTN_WIDTH).astype(BF16)
    y_sample = _merge_ffn(xs, mps, gas, aas, wap, wout, ln2r, wgu, wd, lnfr, bd).reshape(bd, 1, D_MODEL)

    return (y_prompt, y_sample,
            k.reshape(1, nb, seq, N_KV_HEADS, HEAD_DIM), v.reshape(1, nb, seq, N_KV_HEADS, HEAD_DIM),
            ki.reshape(1, nb, seq, IDX_DIM), ul[None, :, 1:, :],
            ks.reshape(1, bd, 1, N_KV_HEADS, HEAD_DIM), vs.reshape(1, bd, 1, N_KV_HEADS, HEAD_DIM),
            kin.reshape(1, bd, 1, IDX_DIM),
            jnp.concatenate([state_pool[0][:, 1:], us[:, None, :]], axis=1)[None])
```

```python
import functools
import math

import jax
import jax.numpy as jnp
import numpy as np
from jax import lax
from jax.experimental import pallas as pl
from jax.experimental.pallas import tpu as pltpu

D_MODEL = 1024
PAGE_SIZE = 128
POOL_WIDTH = D_MODEL // 2
POOL_WINDOWS = (2, 4, 8, 16)
POOL_GROUP_DIM = POOL_WIDTH // len(POOL_WINDOWS)
POOL_BUF = 15
N_HEADS = 8
N_KV_HEADS = 4
HEAD_DIM = 64
GROUP = N_HEADS // N_KV_HEADS
ATTN_WIDTH = N_HEADS * HEAD_DIM
KV_WIDTH = N_KV_HEADS * HEAD_DIM
IDX_HEADS = 4
IDX_DIM = 64
TOPK_MAX = 256
N_BUCKETS = 32
MAX_DISTANCE = 128
D_FF = -(-(8 * D_MODEL) // (3 * 256)) * 256
EPS = 1e-6
IN_SPLITS = (POOL_WIDTH, ATTN_WIDTH, KV_WIDTH, KV_WIDTH, IDX_HEADS * IDX_DIM, IDX_DIM, IDX_HEADS, 2 * D_MODEL)

LANES = 128
SUBLANES = 8
QBLK = 128
CHUNK = 512
INT_MIN = -(2 ** 31)
NEG = -1e30
VMEM_LIMIT = 56 * 1024 * 1024
V_AUG = HEAD_DIM + 16

N_U, N_KA, N_KK, N_G = 0, 512, 1024, 1152
N_END = N_G + 2 * D_MODEL
T_Q, T_K, T_V, T_QI, T_KI, T_WI = 0, 512, 768, 1024, 1280, 1344
T_END = T_WI + SUBLANES

F32 = jnp.float32
BF16 = jnp.bfloat16


def _dot(a, b, precision=None):
    return jnp.dot(a, b, preferred_element_type=F32, precision=precision)


def _dot_nt(a, b):
    return lax.dot_general(a, b, (((1,), (1,)), ((), ())), preferred_element_type=F32)


def _dot_tn(a, b):
    return lax.dot_general(a, b, (((0,), (0,)), ((), ())), preferred_element_type=F32)


def _rms(x, g):
    return (x * lax.rsqrt(jnp.mean(x * x, axis=-1, keepdims=True) + EPS)) * g


def _hi_lo(z):
    hi = z.astype(BF16).astype(F32)
    return hi, z - hi


def _sortable_key(score):
    bits = lax.bitcast_convert_type(score, jnp.int32)
    return jnp.where(bits < 0, -(bits & 0x7FFFFFFF), bits)


def _gated_pool(d_groups, gates, pw_ref, ps_ref, wpp_ref):
    a = [_dot(d.astype(BF16), pw_ref[g]) for g, d in enumerate(d_groups)]
    a = jnp.concatenate(a, axis=1) * ps_ref[...]
    ap = _dot(a.astype(BF16), wpp_ref[...])
    return jax.nn.sigmoid(gates[:, :D_MODEL]) * ap


def _proj_kernel(x_ref, ln1_ref, wn_ref, wt_ref, pw_ref, ps_ref, wpp_ref,
                 qT_ref, kT_ref, vT_ref, vTa_ref, kiT_ref, qsT_ref, wiT_ref, kis_ref, ka_ref, mp_ref, ga_ref, ul_ref,
                 uext_ref, *, tm, tiles_per_seq):
    ti = pl.program_id(0) % tiles_per_seq
    hb = _rms(x_ref[...], ln1_ref[...]).astype(BF16)

    def mm(lo, hi):
        return _dot(hb, wn_ref[:, lo:hi])

    def mt(lo, hi):
        return _dot_nt(wt_ref[lo:hi, :], hb)

    qT_ref[0] = (mt(T_Q, T_K) * HEAD_DIM ** -0.5).astype(BF16)
    kT_ref[0] = mt(T_K, T_V)
    vT = mt(T_V, T_QI)
    vT_ref[0] = vT
    ones = jnp.ones((V_AUG - HEAD_DIM, tm), F32)
    vTa_ref[0] = jnp.concatenate(
        [piece for c in range(N_KV_HEADS) for piece in (vT[c * HEAD_DIM:(c + 1) * HEAD_DIM], ones)], axis=0).astype(BF16)
    hi, lo = _hi_lo(mt(T_QI, T_KI))
    qsT_ref[0] = jnp.concatenate(
        [piece for h in range(IDX_HEADS) for piece in (hi[h * IDX_DIM:(h + 1) * IDX_DIM], lo[h * IDX_DIM:(h + 1) * IDX_DIM])],
        axis=0).astype(BF16)
    kiT_ref[0] = mt(T_KI, T_WI)
    wiT_ref[0] = mt(T_WI, T_END) * IDX_HEADS ** -0.5

    hi, lo = _hi_lo(mm(N_KK, N_G))
    kis_ref[...] = jnp.concatenate([hi, lo], axis=1).astype(BF16)
    lane = lax.broadcasted_iota(jnp.int32, (1, N_KV_HEADS * LANES), 1) % LANES
    bias_cols = jnp.where((lane == HEAD_DIM) | (lane == HEAD_DIM + 1), 1.0, 0.0)
    ka_ref[...] = (mm(N_KA, N_KK) + bias_cols).astype(BF16)

    u = mm(N_U, N_KA)

    @pl.when(ti == 0)
    def _():
        uext_ref[0:16, :] = jnp.zeros((16, POOL_WIDTH), F32)

    @pl.when(ti != 0)
    def _():
        uext_ref[0:16, :] = uext_ref[tm:tm + 16, :]

    uext_ref[16:16 + tm, :] = u
    ul_ref[0] = uext_ref[tm:tm + 16, :]
    pos1 = ti * tm + lax.broadcasted_iota(jnp.int32, (tm, 1), 0) + 1
    ds = []
    for g, w in enumerate(POOL_WINDOWS):
        sl = slice(g * POOL_GROUP_DIM, (g + 1) * POOL_GROUP_DIM)
        win = uext_ref[16:16 + tm, sl]
        for d in range(1, w):
            win = win + uext_ref[16 - d:16 - d + tm, sl]
        cnt = jnp.minimum(w, pos1).astype(F32)
        ds.append(win / cnt - u[:, sl])
    gates = mm(N_G, N_END)
    mp_ref[...] = _gated_pool(ds, gates, pw_ref, ps_ref, wpp_ref)
    ga_ref[...] = jax.nn.sigmoid(gates[:, D_MODEL:])


def _sproj_kernel(x_ref, ln1_ref, wn_ref, wt_ref, pw_ref, ps_ref, wpp_ref, st_ref,
                  q_ref, k_ref, v_ref, ki_ref, qi_ref, wi_ref, mp_ref, ga_ref, u_ref, *, n_past):
    hb = _rms(x_ref[...], ln1_ref[...]).astype(BF16)

    def mm(lo, hi):
        return _dot(hb, wn_ref[:, lo:hi])

    def mn(lo, hi):
        return _dot_nt(hb, wt_ref[lo:hi, :])

    q_ref[...] = (mn(T_Q, T_K) * HEAD_DIM ** -0.5).astype(BF16)
    k_ref[...] = mn(T_K, T_V)
    v_ref[...] = mn(T_V, T_QI)
    qi_ref[...] = mn(T_QI, T_KI)
    ki_ref[...] = mn(T_KI, T_WI)
    wi_ref[...] = mn(T_WI, T_END) * IDX_HEADS ** -0.5
    u = mm(N_U, N_KA)
    u_ref[...] = u
    ds = []
    for g, w in enumerate(POOL_WINDOWS):
        sl = slice(g * POOL_GROUP_DIM, (g + 1) * POOL_GROUP_DIM)
        win = u[:, sl]
        for d in range(1, w):
            win = win + st_ref[POOL_BUF - d][:, sl]
        ds.append(win / float(min(w, n_past + 1)) - u[:, sl])
    gates = mm(N_G, N_END)
    mp_ref[...] = _gated_pool(ds, gates, pw_ref, ps_ref, wpp_ref)
    ga_ref[...] = jax.nn.sigmoid(gates[:, D_MODEL:])


def _bias_kernel(rb_ref, corr_ref, brow_ref, srow_ref):
    i = lax.broadcasted_iota(jnp.int32, (QBLK, LANES), 0)
    j = lax.broadcasted_iota(jnp.int32, (QBLK, LANES), 1)
    max_exact = N_BUCKETS // 2

    def bias_of(dist, h):
        d = jnp.maximum(dist, 1).astype(F32)
        large = max_exact + (jnp.log(d / max_exact) / math.log(MAX_DISTANCE / max_exact)
                             * (N_BUCKETS - max_exact)).astype(jnp.int32)
        bucket = jnp.where(dist < max_exact, dist, jnp.minimum(large, N_BUCKETS - 1))
        acc = jnp.zeros(dist.shape, F32)
        for b in range(N_BUCKETS):
            acc = jnp.where(bucket == b, rb_ref[b, h], acc)
        return acc

    row = lax.broadcasted_iota(jnp.int32, (HEAD_DIM, LANES), 0)
    lane1 = lax.broadcasted_iota(jnp.int32, (1, LANES), 1)
    for h in range(N_HEADS):
        far = rb_ref[N_BUCKETS - 1, h]
        for t in range(2):
            corr_ref[t, h] = bias_of(jnp.maximum(t * LANES + j - i, 0), h) - far
        corr_ref[2, h] = jnp.zeros((QBLK, LANES), F32)
        far_v = jnp.full((HEAD_DIM, LANES), far, F32)
        hi, lo = _hi_lo(far_v)
        brow_ref[h] = jnp.where(row == 0, hi, jnp.where(row == 1, lo, 0.0))
        srow_ref[h:h + 1, :] = jnp.concatenate([bias_of(LANES - lane1, h), bias_of(jnp.zeros((1, LANES), jnp.int32), h),
                                       jnp.full((1, LANES), far, F32)], axis=1)


def _topk_mask_cols(keys_ref, madd_ref, nc, cw, k_top):
    part = 4 * SUBLANES

    def count_ge(thr):
        def body(c, acc):
            kk = keys_ref[pl.ds(pl.multiple_of(c * cw, cw), cw), :]
            ge = jnp.where(kk >= thr, 1, 0)
            return acc + jnp.sum(ge.reshape(cw // part, part, LANES), axis=0)

        acc = lax.fori_loop(0, nc, body, jnp.zeros((part, LANES), jnp.int32))
        return jnp.sum(acc, axis=0, keepdims=True)

    def bit_step(it, thr):
        cand = thr + lax.shift_left(jnp.int32(1), 31 - it)
        return jnp.where(count_ge(cand) >= k_top, cand, thr)

    thr = lax.fori_loop(0, 32, bit_step, jnp.full((1, LANES), INT_MIN, jnp.int32))
    thr_eff = jnp.maximum(thr, INT_MIN + 1)

    def mask_body(c, acc):
        off = pl.multiple_of(c * cw, cw)
        sel = keys_ref[pl.ds(off, cw), :] >= thr_eff
        madd_ref[pl.ds(off, cw), :] = jnp.where(sel, 0.0, NEG)
        return acc + jnp.sum(jnp.where(sel, 1, 0).reshape(cw // part, part, LANES), axis=0)

    acc = lax.fori_loop(0, nc, mask_body, jnp.zeros((part, LANES), jnp.int32))
    n_sel = jnp.sum(acc, axis=0, keepdims=True)

    @pl.when(jnp.max(n_sel) > k_top)
    def _():
        room = (k_top - count_ge(thr + 1)).astype(F32)
        a = lax.broadcasted_iota(jnp.int32, (LANES, LANES), 0)
        b = lax.broadcasted_iota(jnp.int32, (LANES, LANES), 1)
        lower = jnp.where(b <= a, 1.0, 0.0).astype(BF16)

        def tie_body(c, seen):
            off = pl.multiple_of(c * LANES, LANES)
            kk = keys_ref[pl.ds(off, LANES), :]
            eq = kk == thr
            ones = jnp.where(eq, 1.0, 0.0)
            rank = _dot(lower, ones.astype(BF16)) + seen
            tied = jnp.where(rank <= room, 0.0, NEG)
            tied = jnp.where(kk > INT_MIN, tied, NEG)
            madd_ref[pl.ds(off, LANES), :] = jnp.where(eq, tied, jnp.where(kk > thr, 0.0, NEG))
            return seen + jnp.sum(ones, axis=0, keepdims=True)

        lax.fori_loop(0, nc * (cw // LANES), tie_body, jnp.zeros((1, LANES), F32))


def _topk_mask_rows(keys_ref, madd_ref, nc, cw, k_top):
    rows = keys_ref.shape[0]
    sub = cw // LANES

    def count_ge(thr):
        thr_b = jnp.broadcast_to(thr, (rows, LANES))

        def body(c, acc):
            off = pl.multiple_of(c * cw, LANES)
            for s in range(sub):
                kk = keys_ref[:, pl.ds(off + s * LANES, LANES)]
                acc = acc + jnp.where(kk >= thr_b, 1, 0)
            return acc

        acc = lax.fori_loop(0, nc, body, jnp.zeros((rows, LANES), jnp.int32))
        return jnp.sum(acc, axis=1, keepdims=True)

    def bit_step(it, thr):
        cand = thr + lax.shift_left(jnp.int32(1), 31 - it)
        return jnp.where(count_ge(cand) >= k_top, cand, thr)

    thr = lax.fori_loop(0, 32, bit_step, jnp.full((rows, 1), INT_MIN, jnp.int32))
    thr_b = jnp.broadcast_to(jnp.maximum(thr, INT_MIN + 1), (rows, LANES))

    def mask_body(c, acc):
        off = pl.multiple_of(c * cw, LANES)
        for s in range(sub):
            kk = keys_ref[:, pl.ds(off + s * LANES, LANES)]
            sel = kk >= thr_b
            madd_ref[:, pl.ds(off + s * LANES, LANES)] = jnp.where(sel, 0.0, NEG)
            acc = acc + jnp.where(sel, 1, 0)
        return acc

    acc = lax.fori_loop(0, nc, mask_body, jnp.zeros((rows, LANES), jnp.int32))
    n_sel = jnp.sum(acc, axis=1, keepdims=True)

    @pl.when(jnp.max(n_sel) > k_top)
    def _():
        room = (k_top - count_ge(thr + 1)).astype(F32)
        a = lax.broadcasted_iota(jnp.int32, (LANES, LANES), 0)
        b = lax.broadcasted_iota(jnp.int32, (LANES, LANES), 1)
        upper = jnp.where(a <= b, 1.0, 0.0).astype(BF16)
        eq_thr = jnp.broadcast_to(thr, (rows, LANES))

        def tie_body(c, seen):
            off = pl.multiple_of(c * LANES, LANES)
            kk = keys_ref[:, pl.ds(off, LANES)]
            eq = kk == eq_thr
            ones = jnp.where(eq, 1.0, 0.0)
            rank = _dot(ones.astype(BF16), upper) + seen
            tied = jnp.where(rank <= room, 0.0, NEG)
            tied = jnp.where(kk > INT_MIN, tied, NEG)
            madd_ref[:, pl.ds(off, LANES)] = jnp.where(eq, tied, jnp.where(kk > eq_thr, 0.0, NEG))
            return seen + jnp.sum(ones, axis=1, keepdims=True)

        lax.fori_loop(0, nc * sub, tie_body, jnp.zeros((rows, 1), F32))


def _attn_kernel(qT_ref, qsT_ref, wiT_ref, kis_ref, ka_ref, vTa_ref, corr_ref, brow_ref, o_ref,
                 keys_ref, madd_ref, m_ref, acc_ref, s_ref, p_ref, qs_ref, qa_ref, *, k_top):
    j = pl.program_id(1)
    t0 = j * QBLK
    cj = t0 // CHUNK
    nc = cj + 1
    qpos = t0 + lax.broadcasted_iota(jnp.int32, (1, QBLK), 1)

    for h in range(IDX_HEADS):
        qs_ref[h] = jnp.concatenate([qsT_ref[0, h * LANES:(h + 1) * LANES, :]] * 2, axis=0)

    def score_body(c, carry):
        off = pl.multiple_of(c * CHUNK, CHUNK)
        kc = kis_ref[pl.ds(off, CHUNK), :]
        sc = jnp.zeros((CHUNK, QBLK), F32)
        for h in range(IDX_HEADS):
            s = _dot(kc, qs_ref[h]) * IDX_DIM ** -0.5
            sc = sc + wiT_ref[0, h:h + 1, :] * jnp.maximum(s, 0.0)
        kpos = off + lax.broadcasted_iota(jnp.int32, (CHUNK, 1), 0)
        keys_ref[pl.ds(off, CHUNK), :] = jnp.where(kpos <= qpos, _sortable_key(sc), INT_MIN)
        return carry

    lax.fori_loop(0, nc, score_body, 0)
    _topk_mask_cols(keys_ref, madd_ref, nc, CHUNK, k_top)

    for g in range(N_KV_HEADS):
        heads = [jnp.concatenate([qT_ref[0, h * HEAD_DIM:(h + 1) * HEAD_DIM, :], brow_ref[h].astype(BF16)], axis=0)
                 for h in range(GROUP * g, GROUP * (g + 1))]
        qa_ref[g] = jnp.concatenate(heads, axis=1)
    m_ref[...] = jnp.full(m_ref.shape, NEG, F32)
    acc_ref[...] = jnp.zeros(acc_ref.shape, F32)

    def chunk_step(c, near):
        off = pl.multiple_of(c * CHUNK, CHUNK)
        nblk = CHUNK // LANES
        col_max = []
        for g in range(N_KV_HEADS):
            mx = jnp.full((SUBLANES, GROUP * QBLK), NEG, F32)
            for i in range(nblk):
                rows = pl.ds(off + i * LANES, LANES)
                s = _dot(ka_ref[rows, g * LANES:(g + 1) * LANES], qa_ref[g])
                s = s + jnp.concatenate([madd_ref[rows, :]] * GROUP, axis=1)
                if near:
                    tile = jnp.clip(j - (c * nblk + i), 0, 2)
                    s = s + jnp.concatenate([corr_ref[tile, h] for h in range(GROUP * g, GROUP * (g + 1))], axis=1)
                s_ref[g, i * LANES:(i + 1) * LANES, :] = s
                mx = jnp.maximum(mx, jnp.max(s.reshape(LANES // SUBLANES, SUBLANES, GROUP * QBLK), axis=0))
            col_max.append(jnp.max(mx, axis=0, keepdims=True))
        for g in range(N_KV_HEADS):
            m_prev = m_ref[g]
            m_next = jnp.maximum(m_prev, col_max[g])
            for i in range(nblk):
                blk = slice(i * LANES, (i + 1) * LANES)
                p_ref[g, blk, :] = jnp.exp(s_ref[g, blk, :] - m_next).astype(BF16)
            pv = _dot(vTa_ref[0, g * V_AUG:(g + 1) * V_AUG, pl.ds(off, CHUNK)], p_ref[g])
            acc_ref[g] = jnp.exp(m_prev - m_next) * acc_ref[g] + pv
            m_ref[g] = m_next

    def far_body(c, carry):
        chunk_step(c, False)
        return carry

    lax.fori_loop(0, jnp.maximum(cj - 1, 0), far_body, 0)

    @pl.when(cj >= 1)
    def _():
        chunk_step(cj - 1, True)

    chunk_step(cj, True)
    for g in range(N_KV_HEADS):
        acc = acc_ref[g]
        out = acc[0:HEAD_DIM] / acc[HEAD_DIM:HEAD_DIM + 1]
        for r in range(GROUP):
            h = GROUP * g + r
            o_ref[0, h * HEAD_DIM:(h + 1) * HEAD_DIM, :] = out[:, r * QBLK:(r + 1) * QBLK].astype(o_ref.dtype)


def _ffn_kernel(x_ref, mp_ref, ga_ref, aaT_ref, wap_ref, wout_ref, ln2_ref, wgu_ref, wd_ref, lnf_ref, y_ref, *, fc):
    att = _dot_tn(aaT_ref[0], wap_ref[...])
    m = mp_ref[...] + ga_ref[...] * att
    h = x_ref[...] + _dot(m.astype(BF16), wout_ref[...])
    hn = _rms(h, ln2_ref[...]).astype(BF16)
    y = h
    for c in range(D_FF // fc):
        gate = _dot(hn, wgu_ref[:, c * fc:(c + 1) * fc])
        up = _dot(hn, wgu_ref[:, D_FF + c * fc:D_FF + (c + 1) * fc])
        act = (gate * jax.nn.sigmoid(gate)) * up
        y = y + _dot(act.astype(BF16), wd_ref[c * fc:(c + 1) * fc, :])
    y_ref[...] = _rms(y, lnf_ref[...])


def _sscore_kernel(pt_ref, qi_ref, wi_ref, *refs, pages):
    page_refs, o_ref = refs[:pages], refs[pages]
    kc = jnp.concatenate([r[0, 0] for r in page_refs], axis=1)
    s = _dot(qi_ref[0], kc, precision=lax.Precision.HIGHEST) * IDX_DIM ** -0.5
    o_ref[0] = jnp.sum(wi_ref[0] * jnp.maximum(s, 0.0), axis=0, keepdims=True)


def _sselect_kernel(sc_ref, qi_ref, kin_ref, wi_ref, madd_ref, keys_ref, *, n_past, cw, k_top):
    rows = sc_ref.shape[0]
    keys_ref[:, 0:n_past] = _sortable_key(sc_ref[...])
    kin = kin_ref[...]
    wi = wi_ref[...]
    s_new = jnp.zeros((rows, 1), F32)
    for h in range(IDX_HEADS):
        s = jnp.sum(qi_ref[:, h * IDX_DIM:(h + 1) * IDX_DIM] * kin, axis=1, keepdims=True) * IDX_DIM ** -0.5
        s_new = s_new + wi[:, h:h + 1] * jnp.maximum(s, 0.0)
    lane = lax.broadcasted_iota(jnp.int32, (rows, keys_ref.shape[1] - n_past), 1)
    keys_ref[:, n_past:] = jnp.where(lane == 0, _sortable_key(s_new), INT_MIN)
    _topk_mask_rows(keys_ref, madd_ref, keys_ref.shape[1] // cw, cw, k_top)


def _sattn_kernel(pt_ref, qbd_ref, madd_ref, maddn_ref, sb_ref, sbn_ref, kn_ref, vn_ref, *refs, pages):
    k_refs, v_refs = refs[:pages], refs[pages:2 * pages]
    o_ref, m_ref, l_ref, acc_ref = refs[2 * pages:]
    c = pl.program_id(1)

    @pl.when(c == 0)
    def _():
        m_ref[...] = jnp.full(m_ref.shape, NEG, F32)
        l_ref[...] = jnp.zeros(l_ref.shape, F32)
        acc_ref[...] = jnp.zeros(acc_ref.shape, F32)

    qbd = qbd_ref[0]
    kc = jnp.concatenate([r[0, 0].reshape(KV_WIDTH, PAGE_SIZE) for r in k_refs], axis=1).astype(BF16)
    vc = jnp.concatenate([r[0, 0].reshape(KV_WIDTH, PAGE_SIZE) for r in v_refs], axis=1).astype(BF16)
    s = _dot(qbd, kc) + sb_ref[...] + madd_ref[0]
    m_prev = m_ref[...]
    m_next = jnp.maximum(m_prev, jnp.max(s, axis=1, keepdims=True))
    alpha = jnp.exp(m_prev - m_next)
    p = jnp.exp(s - m_next)
    l_ref[...] = alpha * l_ref[...] + jnp.sum(p, axis=1, keepdims=True)
    acc_ref[...] = alpha * acc_ref[...] + _dot_nt(p.astype(BF16), vc)
    m_ref[...] = m_next

    @pl.when(c == pl.num_programs(1) - 1)
    def _():
        s_new = (jnp.sum(qbd.astype(F32) * kn_ref[0], axis=1, keepdims=True)
                 + sbn_ref[:, 0:1] + maddn_ref[0][:, 0:1])
        m_prev = m_ref[...]
        m_next = jnp.maximum(m_prev, s_new)
        alpha = jnp.exp(m_prev - m_next)
        p = jnp.exp(s_new - m_next)
        l = alpha * l_ref[...] + p
        o_ref[0] = (alpha * acc_ref[...] + p * vn_ref[0]) / l


def _full(shape):
    return pl.BlockSpec(shape, lambda *_: (0,) * len(shape))


def _pack_w_in(w):
    offs = np.cumsum((0,) + IN_SPLITS)
    wu, wq, wk, wv, wqi, wki, wwi, wg = [w[:, offs[i]:offs[i + 1]] for i in range(len(IN_SPLITS))]
    pad = jnp.zeros((D_MODEL, LANES - HEAD_DIM), w.dtype)
    wka = jnp.concatenate([piece for c in range(N_KV_HEADS) for piece in (wk[:, c * HEAD_DIM:(c + 1) * HEAD_DIM], pad)], axis=1)
    w_nat = jnp.concatenate([wu, wka, wki, wki, wg], axis=1).astype(BF16)
    wwp = jnp.pad(wwi, ((0, 0), (0, SUBLANES - IDX_HEADS)))
    w_t = jnp.concatenate([wq, wk, wv, wqi, wki, wwp], axis=1).T.astype(BF16)
    return w_nat, w_t


def _proj_prompt(x2, ln1, w_nat, w_t, pw, ps, wpp, nb, seq, tm):
    t = x2.shape[0]
    tps = seq // tm
    row = lambda n: pl.BlockSpec((tm, n), lambda i: (i, 0))
    col = lambda n: pl.BlockSpec((1, n, tm), lambda i: (i // tps, 0, i % tps))
    tshape = lambda n, dt: jax.ShapeDtypeStruct((nb, n, seq), dt)
    out_shape = (
        tshape(ATTN_WIDTH, BF16),
        tshape(KV_WIDTH, F32), tshape(KV_WIDTH, F32),
        tshape(N_KV_HEADS * V_AUG, BF16),
        tshape(IDX_DIM, F32),
        tshape(IDX_HEADS * LANES, BF16),
        tshape(SUBLANES, F32),
        jax.ShapeDtypeStruct((t, 2 * LANES), BF16),
        jax.ShapeDtypeStruct((t, N_KV_HEADS * LANES), BF16),
        jax.ShapeDtypeStruct((t, D_MODEL), F32),
        jax.ShapeDtypeStruct((t, D_MODEL), F32),
        jax.ShapeDtypeStruct((nb, 16, POOL_WIDTH), F32),
    )
    out_specs = (col(ATTN_WIDTH), col(KV_WIDTH), col(KV_WIDTH), col(N_KV_HEADS * V_AUG), col(IDX_DIM),
                 col(IDX_HEADS * LANES), col(SUBLANES), row(2 * LANES), row(N_KV_HEADS * LANES), row(D_MODEL),
                 row(D_MODEL), pl.BlockSpec((1, 16, POOL_WIDTH), lambda i: (i // tps, 0, 0)))
    return pl.pallas_call(
        functools.partial(_proj_kernel, tm=tm, tiles_per_seq=tps),
        grid=(t // tm,),
        in_specs=[row(D_MODEL), _full((1, D_MODEL)), _full(w_nat.shape), _full(w_t.shape), _full(pw.shape),
                  _full((1, POOL_WIDTH)), _full(wpp.shape)],
        out_specs=out_specs,
        out_shape=out_shape,
        scratch_shapes=[pltpu.VMEM((tm + 16, POOL_WIDTH), F32)],
        compiler_params=pltpu.CompilerParams(dimension_semantics=("arbitrary",), vmem_limit_bytes=VMEM_LIMIT),
        name="proj_pool",
    )(x2, ln1, w_nat, w_t, pw, ps, wpp)


def _proj_sample(x2, ln1, w_nat, w_t, pw, ps, wpp, st, n_past):
    m = x2.shape[0]
    shapes = [(ATTN_WIDTH, BF16), (KV_WIDTH, F32), (KV_WIDTH, F32), (IDX_DIM, F32), (IDX_HEADS * IDX_DIM, F32),
              (SUBLANES, F32), (D_MODEL, F32), (D_MODEL, F32), (POOL_WIDTH, F32)]
    return pl.pallas_call(
        functools.partial(_sproj_kernel, n_past=n_past),
        out_shape=tuple(jax.ShapeDtypeStruct((m, n), dt) for n, dt in shapes),
        compiler_params=pltpu.CompilerParams(vmem_limit_bytes=VMEM_LIMIT),
        name="proj_pool_sample",
    )(x2, ln1, w_nat, w_t, pw, ps, wpp, st)


def _bias_tables(rel_bias):
    return pl.pallas_call(
        _bias_kernel,
        in_specs=[pl.BlockSpec(memory_space=pltpu.SMEM)],
        out_shape=(jax.ShapeDtypeStruct((3, N_HEADS, QBLK, LANES), F32),
                   jax.ShapeDtypeStruct((N_HEADS, HEAD_DIM, LANES), F32),
                   jax.ShapeDtypeStruct((N_HEADS, 3 * LANES), F32)),
        name="bias_tables",
    )(rel_bias)


def _attn_prompt(qT, qsT, wiT, kis, ka, vTa, corr, brow, nb, seq, k_top):
    nq = seq // QBLK
    qcol = lambda n: pl.BlockSpec((1, n, QBLK), lambda b, j: (b, 0, j))
    seqrow = lambda n: pl.BlockSpec((seq, n), lambda b, j: (b, 0))
    return pl.pallas_call(
        functools.partial(_attn_kernel, k_top=k_top),
        grid=(nb, nq),
        in_specs=[qcol(ATTN_WIDTH), qcol(IDX_HEADS * LANES), qcol(SUBLANES), seqrow(2 * LANES), seqrow(N_KV_HEADS * LANES),
                  pl.BlockSpec((1, N_KV_HEADS * V_AUG, seq), lambda b, j: (b, 0, 0)),
                  _full(corr.shape), _full(brow.shape)],
        out_specs=qcol(ATTN_WIDTH),
        out_shape=jax.ShapeDtypeStruct((nb, ATTN_WIDTH, seq), BF16),
        scratch_shapes=[pltpu.VMEM((seq, QBLK), jnp.int32), pltpu.VMEM((seq, QBLK), F32),
                        pltpu.VMEM((N_KV_HEADS, 1, GROUP * QBLK), F32),
                        pltpu.VMEM((N_KV_HEADS, V_AUG, GROUP * QBLK), F32),
                        pltpu.VMEM((N_KV_HEADS, CHUNK, GROUP * QBLK), F32),
                        pltpu.VMEM((N_KV_HEADS, CHUNK, GROUP * QBLK), BF16),
                        pltpu.VMEM((IDX_HEADS, 4 * IDX_DIM, QBLK), BF16),
                        pltpu.VMEM((N_KV_HEADS, LANES, GROUP * QBLK), BF16)],
        compiler_params=pltpu.CompilerParams(dimension_semantics=("arbitrary", "arbitrary"),
                                             vmem_limit_bytes=VMEM_LIMIT),
        name="sparse_attn",
    )(qT, qsT, wiT, kis, ka, vTa, corr, brow)


def _merge_ffn(x2, mp, ga, aaT, wap, wout, ln2, wgu, wd, lnf, seq, tm):
    t = x2.shape[0]
    tps = seq // tm
    row = lambda n: pl.BlockSpec((tm, n), lambda i: (i, 0))
    return pl.pallas_call(
        functools.partial(_ffn_kernel, fc=256),
        grid=(t // tm,),
        in_specs=[row(D_MODEL), row(D_MODEL), row(D_MODEL),
                  pl.BlockSpec((1, ATTN_WIDTH, tm), lambda i: (i // tps, 0, i % tps)),
                  _full(wap.shape), _full(wout.shape), _full((1, D_MODEL)), _full(wgu.shape), _full(wd.shape),
                  _full((1, D_MODEL))],
        out_specs=row(D_MODEL),
        out_shape=jax.ShapeDtypeStruct((t, D_MODEL), F32),
        compiler_params=pltpu.CompilerParams(dimension_semantics=("arbitrary",), vmem_limit_bytes=VMEM_LIMIT),
        name="merge_ffn",
    )(x2, mp, ga, aaT, wap, wout, ln2, wgu, wd, lnf)


def _sample_scores(page_table, qi, wi4, cache_ikT, pages):
    bd, n_pages = page_table.shape
    page_spec = lambda i: pl.BlockSpec((1, 1, IDX_DIM, PAGE_SIZE), lambda b, c, pt: (0, pt[b, c * pages + i], 0, 0))
    grid_spec = pltpu.PrefetchScalarGridSpec(
        num_scalar_prefetch=1,
        grid=(bd, n_pages // pages),
        in_specs=[pl.BlockSpec((1, IDX_HEADS, IDX_DIM), lambda b, c, pt: (b, 0, 0)),
                  pl.BlockSpec((1, IDX_HEADS, 1), lambda b, c, pt: (b, 0, 0))] + [page_spec(i) for i in range(pages)],
        out_specs=pl.BlockSpec((1, 1, pages * PAGE_SIZE), lambda b, c, pt: (b, 0, c)),
    )
    return pl.pallas_call(
        functools.partial(_sscore_kernel, pages=pages),
        grid_spec=grid_spec,
        out_shape=jax.ShapeDtypeStruct((bd, 1, n_pages * PAGE_SIZE), F32),
        compiler_params=pltpu.CompilerParams(dimension_semantics=("arbitrary", "arbitrary")),
        name="sample_scores",
    )(page_table, qi, wi4, *([cache_ikT] * pages))


def _sample_select(scores, qi, ki_new, wi, n_past, cw, k_top):
    bd = scores.shape[0]
    width = n_past + LANES
    return pl.pallas_call(
        functools.partial(_sselect_kernel, n_past=n_past, cw=cw, k_top=k_top),
        out_shape=jax.ShapeDtypeStruct((bd, width), F32),
        scratch_shapes=[pltpu.VMEM((bd, width), jnp.int32)],
        compiler_params=pltpu.CompilerParams(vmem_limit_bytes=VMEM_LIMIT),
        name="sample_select",
    )(scores, qi, ki_new, wi)


def _sample_attn(page_table, qbd, madd, madd_new, sbias, sbias_new, k_new, v_new, cache_kT, cache_vT, pages):
    bd, n_pages = page_table.shape
    chunk = pages * PAGE_SIZE
    page_spec = lambda i: pl.BlockSpec((1, 1, N_KV_HEADS, HEAD_DIM, PAGE_SIZE),
                                       lambda b, c, pt: (0, pt[b, c * pages + i], 0, 0, 0))
    per_b = lambda shape: pl.BlockSpec((1,) + shape, lambda b, c, pt: (b, 0, 0))
    grid_spec = pltpu.PrefetchScalarGridSpec(
        num_scalar_prefetch=1,
        grid=(bd, n_pages // pages),
        in_specs=[per_b((N_HEADS, KV_WIDTH)),
                  pl.BlockSpec((1, 1, chunk), lambda b, c, pt: (b, 0, c)),
                  per_b((1, LANES)),
                  pl.BlockSpec((N_HEADS, chunk), lambda b, c, pt: (0, c)),
                  pl.BlockSpec((N_HEADS, LANES), lambda b, c, pt: (0, 0)),
                  per_b((1, KV_WIDTH)), per_b((1, KV_WIDTH))]
                 + [page_spec(i) for i in range(pages)] * 2,
        out_specs=per_b((N_HEADS, KV_WIDTH)),
        scratch_shapes=[pltpu.VMEM((N_HEADS, 1), F32), pltpu.VMEM((N_HEADS, 1), F32),
                        pltpu.VMEM((N_HEADS, KV_WIDTH), F32)],
    )
    return pl.pallas_call(
        functools.partial(_sattn_kernel, pages=pages),
        grid_spec=grid_spec,
        out_shape=jax.ShapeDtypeStruct((bd, N_HEADS, KV_WIDTH), F32),
        compiler_params=pltpu.CompilerParams(dimension_semantics=("arbitrary", "arbitrary"),
                                             vmem_limit_bytes=VMEM_LIMIT),
        name="sample_attn",
    )(page_table, qbd, madd, madd_new, sbias, sbias_new, k_new, v_new, *([cache_kT] * pages), *([cache_vT] * pages))


def _pick(n, prefs):
    for p in prefs:
        if n % p == 0:
            return p
    return n


def kernel(x_prompt, x_sample, cache_k, cache_v, cache_idx_k, state_pool, page_table, ln1, w_in, pool_w, pool_scale,
           w_pool_proj, w_attn_proj, w_out, ln2, w_gate_up, w_down, rel_bias, ln_final):
    assert w_in.shape[0] == 1, "one layer"
    nb, seq, _ = x_prompt.shape
    bd, dec_seq, _ = x_sample.shape
    assert dec_seq == 1 and seq % CHUNK == 0
    n_pages = page_table.shape[1]
    n_past = n_pages * PAGE_SIZE

    w_nat, w_t = _pack_w_in(w_in[0])
    ln1r, ln2r, lnfr = ln1[0][None], ln2[0][None], ln_final[None]
    pw = pool_w[0].astype(BF16)
    ps = pool_scale[0][None]
    wpp, wap, wout = w_pool_proj[0].astype(BF16), w_attn_proj[0].astype(BF16), w_out[0].astype(BF16)
    wgu, wd = w_gate_up[0].astype(BF16), w_down[0].astype(BF16)
    corr, brow, srow = _bias_tables(rel_bias)

    xp = x_prompt.reshape(nb * seq, D_MODEL)
    qT, kT, vT, vTa, kiT, qsT, wiT, kis, ka, mp, ga, ul = _proj_prompt(
        xp, ln1r, w_nat, w_t, pw, ps, wpp, nb, seq, _pick(seq, (512, 256, 128)))
    aaT = _attn_prompt(qT, qsT, wiT, kis, ka, vTa, corr, brow, nb, seq, min(TOPK_MAX, seq // 4))
    y_prompt = _merge_ffn(xp, mp, ga, aaT, wap, wout, ln2r, wgu, wd, lnfr, seq, _pick(seq, (256, 128)))
    y_prompt = y_prompt.reshape(nb, seq, D_MODEL)

    xs = x_sample.reshape(bd, D_MODEL)
    st = jnp.swapaxes(state_pool[0], 0, 1)
    qs, ks, vs, kin, qin, wis, mps, gas, us = _proj_sample(xs, ln1r, w_nat, w_t, pw, ps, wpp, st, n_past)
    pages = _pick(n_pages, (16, 8, 4, 2))
    scores = _sample_scores(page_table, qin.reshape(bd, IDX_HEADS, IDX_DIM), wis[:, :IDX_HEADS, None],
                            jnp.swapaxes(cache_idx_k, 2, 3), pages).reshape(bd, n_past)
    width = n_past + LANES
    madd = _sample_select(scores, qin, kin, wis, n_past, _pick(width, (640, 512, 384, 256, 128)),
                          min(TOPK_MAX, (n_past + 1) // 4))
    head_kv = np.arange(N_HEADS) // GROUP
    onehot = jnp.asarray(head_kv[:, None] == np.arange(N_KV_HEADS)[None, :])
    qbd = jnp.where(onehot[None, :, :, None], qs.reshape(bd, N_HEADS, 1, HEAD_DIM), 0).reshape(bd, N_HEADS, KV_WIDTH)
    far = jnp.broadcast_to(srow[:, 2 * LANES:2 * LANES + 1], (N_HEADS, n_past - LANES))
    sbias = jnp.concatenate([far, srow[:, :LANES]], axis=1)
    ao = _sample_attn(page_table, qbd, madd[:, None, :n_past], madd[:, None, n_past:], sbias, srow[:, LANES:2 * LANES],
                      ks[:, None, :], vs[:, None, :], jnp.transpose(cache_k, (0, 1, 3, 4, 2)),
                      jnp.transpose(cache_v, (0, 1, 3, 4, 2)), pages)
    aas = jnp.sum(jnp.where(onehot[None, :, :, None], ao.reshape(bd, N_HEADS, N_KV_HEADS, HEAD_DIM), 0), axis=2)
    aasT = aas.reshape(bd, ATTN_WIDTH).astype(BF16).T[None]
    y_sample = _merge_ffn(xs, mps, gas, aasT, wap, wout, ln2r, wgu, wd, lnfr, bd, bd).reshape(bd, 1, D_MODEL)

    to_heads = lambda a: jnp.transpose(a.reshape(nb, N_KV_HEADS, HEAD_DIM, seq), (0, 3, 1, 2))[None]
    return (y_prompt, y_sample, to_heads(kT), to_heads(vT), jnp.swapaxes(kiT, 1, 2)[None], ul[None, :, 1:, :],
            ks.reshape(1, bd, 1, N_KV_HEADS, HEAD_DIM), vs.reshape(1, bd, 1, N_KV_HEADS, HEAD_DIM),
            kin.reshape(1, bd, 1, IDX_DIM),
            jnp.concatenate([state_pool[0][:, 1:], us[:, None, :]], axis=1)[None])
```

```python
import functools
import math

import jax
import jax.numpy as jnp
import numpy as np
from jax import lax
from jax.experimental import pallas as pl
from jax.experimental.pallas import tpu as pltpu

D_MODEL = 1024
PAGE_SIZE = 128
POOL_WIDTH = D_MODEL // 2
POOL_WINDOWS = (2, 4, 8, 16)
POOL_GROUP_DIM = POOL_WIDTH // len(POOL_WINDOWS)
POOL_BUF = 15
N_HEADS = 8
N_KV_HEADS = 4
HEAD_DIM = 64
GROUP = N_HEADS // N_KV_HEADS
ATTN_WIDTH = N_HEADS * HEAD_DIM
KV_WIDTH = N_KV_HEADS * HEAD_DIM
IDX_HEADS = 4
IDX_DIM = 64
TOPK_MAX = 256
N_BUCKETS = 32
MAX_DISTANCE = 128
D_FF = -(-(8 * D_MODEL) // (3 * 256)) * 256
EPS = 1e-6
IN_SPLITS = (POOL_WIDTH, ATTN_WIDTH, KV_WIDTH, KV_WIDTH, IDX_HEADS * IDX_DIM, IDX_DIM, IDX_HEADS, 2 * D_MODEL)

LANES = 128
SUBLANES = 8
QBLK = 128
CHUNK = 512
INT_MIN = -(2 ** 31)
NEG = -1e30
LOG2E = math.log2(math.e)
VMEM_LIMIT = 56 * 1024 * 1024
V_AUG = HEAD_DIM + 16

N_U, N_KA, N_KK, N_G = 0, 512, 1024, 1152
N_END = N_G + 2 * D_MODEL
T_Q, T_K, T_V, T_QI, T_KI, T_WI = 0, 512, 768, 1024, 1280, 1344
T_END = T_WI + SUBLANES

F32 = jnp.float32
BF16 = jnp.bfloat16


def _dot(a, b, precision=None):
    return jnp.dot(a, b, preferred_element_type=F32, precision=precision)


def _dot_nt(a, b):
    return lax.dot_general(a, b, (((1,), (1,)), ((), ())), preferred_element_type=F32)


def _dot_tn(a, b):
    return lax.dot_general(a, b, (((0,), (0,)), ((), ())), preferred_element_type=F32)


def _rms(x, g):
    return (x * lax.rsqrt(jnp.mean(x * x, axis=-1, keepdims=True) + EPS)) * g


def _hi_lo(z):
    hi = z.astype(BF16).astype(F32)
    return hi, z - hi


def _sortable_key(score):
    bits = lax.bitcast_convert_type(score, jnp.int32)
    return jnp.where(bits < 0, -(bits & 0x7FFFFFFF), bits)


def _gated_pool(d_groups, gates, pw_ref, ps_ref, wpp_ref):
    a = [_dot(d.astype(BF16), pw_ref[g]) for g, d in enumerate(d_groups)]
    a = jnp.concatenate(a, axis=1) * ps_ref[...]
    ap = _dot(a.astype(BF16), wpp_ref[...])
    return jax.nn.sigmoid(gates[:, :D_MODEL]) * ap


def _proj_kernel(x_ref, ln1_ref, wn_ref, wt_ref, pw_ref, ps_ref, wpp_ref,
                 qT_ref, kT_ref, vT_ref, vTa_ref, kiT_ref, qsT_ref, wiT_ref, kis_ref, ka_ref, mp_ref, ga_ref, ul_ref,
                 uext_ref, *, tm, tiles_per_seq):
    ti = pl.program_id(0) % tiles_per_seq
    hb = _rms(x_ref[...], ln1_ref[...]).astype(BF16)

    def mm(lo, hi):
        return _dot(hb, wn_ref[:, lo:hi])

    def mt(lo, hi):
        return _dot_nt(wt_ref[lo:hi, :], hb)

    qT_ref[0] = (mt(T_Q, T_K) * (HEAD_DIM ** -0.5 * LOG2E)).astype(BF16)
    kT_ref[0] = mt(T_K, T_V)
    vT = mt(T_V, T_QI)
    vT_ref[0] = vT
    ones = jnp.ones((V_AUG - HEAD_DIM, tm), F32)
    vTa_ref[0] = jnp.concatenate(
        [piece for c in range(N_KV_HEADS) for piece in (vT[c * HEAD_DIM:(c + 1) * HEAD_DIM], ones)], axis=0).astype(BF16)
    hi, lo = _hi_lo(mt(T_QI, T_KI))
    qsT_ref[0] = jnp.concatenate(
        [piece for h in range(IDX_HEADS) for piece in (hi[h * IDX_DIM:(h + 1) * IDX_DIM], lo[h * IDX_DIM:(h + 1) * IDX_DIM])],
        axis=0).astype(BF16)
    kiT_ref[0] = mt(T_KI, T_WI)
    wiT_ref[0] = mt(T_WI, T_END) * IDX_HEADS ** -0.5

    hi, lo = _hi_lo(mm(N_KK, N_G))
    kis_ref[...] = jnp.concatenate([hi, lo], axis=1).astype(BF16)
    lane = lax.broadcasted_iota(jnp.int32, (1, N_KV_HEADS * LANES), 1) % LANES
    bias_cols = jnp.where((lane == HEAD_DIM) | (lane == HEAD_DIM + 1), 1.0, 0.0)
    ka_ref[...] = (mm(N_KA, N_KK) + bias_cols).astype(BF16)

    u = mm(N_U, N_KA)

    @pl.when(ti == 0)
    def _():
        uext_ref[0:16, :] = jnp.zeros((16, POOL_WIDTH), F32)

    @pl.when(ti != 0)
    def _():
        uext_ref[0:16, :] = uext_ref[tm:tm + 16, :]

    uext_ref[16:16 + tm, :] = u
    ul_ref[0] = uext_ref[tm:tm + 16, :]
    pos1 = ti * tm + lax.broadcasted_iota(jnp.int32, (tm, 1), 0) + 1
    ds = []
    for g, w in enumerate(POOL_WINDOWS):
        sl = slice(g * POOL_GROUP_DIM, (g + 1) * POOL_GROUP_DIM)
        win = uext_ref[16:16 + tm, sl]
        for d in range(1, w):
            win = win + uext_ref[16 - d:16 - d + tm, sl]
        cnt = jnp.minimum(w, pos1).astype(F32)
        ds.append(win / cnt - u[:, sl])
    gates = mm(N_G, N_END)
    mp_ref[...] = _gated_pool(ds, gates, pw_ref, ps_ref, wpp_ref)
    ga_ref[...] = jax.nn.sigmoid(gates[:, D_MODEL:])


def _sproj_kernel(x_ref, ln1_ref, wn_ref, wt_ref, pw_ref, ps_ref, wpp_ref, st_ref,
                  q_ref, k_ref, v_ref, ki_ref, qi_ref, wi_ref, mp_ref, ga_ref, u_ref, *, n_past):
    hb = _rms(x_ref[...], ln1_ref[...]).astype(BF16)

    def mm(lo, hi):
        return _dot(hb, wn_ref[:, lo:hi])

    def mn(lo, hi):
        return _dot_nt(hb, wt_ref[lo:hi, :])

    q_ref[...] = (mn(T_Q, T_K) * HEAD_DIM ** -0.5).astype(BF16)
    k_ref[...] = mn(T_K, T_V)
    v_ref[...] = mn(T_V, T_QI)
    qi_ref[...] = mn(T_QI, T_KI)
    ki_ref[...] = mn(T_KI, T_WI)
    wi_ref[...] = mn(T_WI, T_END) * IDX_HEADS ** -0.5
    u = mm(N_U, N_KA)
    u_ref[...] = u
    ds = []
    for g, w in enumerate(POOL_WINDOWS):
        sl = slice(g * POOL_GROUP_DIM, (g + 1) * POOL_GROUP_DIM)
        win = u[:, sl]
        for d in range(1, w):
            win = win + st_ref[POOL_BUF - d][:, sl]
        ds.append(win / float(min(w, n_past + 1)) - u[:, sl])
    gates = mm(N_G, N_END)
    mp_ref[...] = _gated_pool(ds, gates, pw_ref, ps_ref, wpp_ref)
    ga_ref[...] = jax.nn.sigmoid(gates[:, D_MODEL:])


def _bias_kernel(rb_ref, corr_ref, brow_ref, srow_ref):
    i = lax.broadcasted_iota(jnp.int32, (QBLK, LANES), 0)
    j = lax.broadcasted_iota(jnp.int32, (QBLK, LANES), 1)
    max_exact = N_BUCKETS // 2

    def bias_of(dist, h):
        d = jnp.maximum(dist, 1).astype(F32)
        large = max_exact + (jnp.log(d / max_exact) / math.log(MAX_DISTANCE / max_exact)
                             * (N_BUCKETS - max_exact)).astype(jnp.int32)
        bucket = jnp.where(dist < max_exact, dist, jnp.minimum(large, N_BUCKETS - 1))
        acc = jnp.zeros(dist.shape, F32)
        for b in range(N_BUCKETS):
            acc = jnp.where(bucket == b, rb_ref[b, h], acc)
        return acc

    row = lax.broadcasted_iota(jnp.int32, (HEAD_DIM, LANES), 0)
    lane1 = lax.broadcasted_iota(jnp.int32, (1, LANES), 1)
    for h in range(N_HEADS):
        far = rb_ref[N_BUCKETS - 1, h]
        for t in range(2):
            corr_ref[t, h] = (bias_of(jnp.maximum(t * LANES + j - i, 0), h) - far) * LOG2E
        corr_ref[2, h] = jnp.zeros((QBLK, LANES), F32)
        far_v = jnp.full((HEAD_DIM, LANES), far, F32) * LOG2E
        hi, lo = _hi_lo(far_v)
        brow_ref[h] = jnp.where(row == 0, hi, jnp.where(row == 1, lo, 0.0))
        srow_ref[h:h + 1, :] = jnp.concatenate([bias_of(LANES - lane1, h), bias_of(jnp.zeros((1, LANES), jnp.int32), h),
                                       jnp.full((1, LANES), far, F32)], axis=1)


def _topk_mask_cols(keys_ref, madd_ref, nc, cw, k_top):
    part = 4 * SUBLANES

    def count_ge(thr):
        def body(c, acc):
            kk = keys_ref[pl.ds(pl.multiple_of(c * cw, cw), cw), :]
            for i in range(cw // part):
                acc = jnp.where(kk[i * part:(i + 1) * part] >= thr, acc + 1, acc)
            return acc

        acc = lax.fori_loop(0, nc, body, jnp.zeros((part, LANES), jnp.int32))
        return jnp.sum(acc, axis=0, keepdims=True)

    def bit_step(it, thr):
        cand = thr + lax.shift_left(jnp.int32(1), 31 - it)
        return jnp.where(count_ge(cand) >= k_top, cand, thr)

    thr = lax.fori_loop(0, 32, bit_step, jnp.full((1, LANES), INT_MIN, jnp.int32))
    thr_eff = jnp.maximum(thr, INT_MIN + 1)

    def mask_body(c, acc):
        off = pl.multiple_of(c * cw, cw)
        sel = keys_ref[pl.ds(off, cw), :] >= thr_eff
        madd_ref[pl.ds(off, cw), :] = jnp.where(sel, 0.0, NEG)
        return acc + jnp.sum(jnp.where(sel, 1, 0).reshape(cw // part, part, LANES), axis=0)

    acc = lax.fori_loop(0, nc, mask_body, jnp.zeros((part, LANES), jnp.int32))
    n_sel = jnp.sum(acc, axis=0, keepdims=True)

    @pl.when(jnp.max(n_sel) > k_top)
    def _():
        room = (k_top - count_ge(thr + 1)).astype(F32)
        a = lax.broadcasted_iota(jnp.int32, (LANES, LANES), 0)
        b = lax.broadcasted_iota(jnp.int32, (LANES, LANES), 1)
        lower = jnp.where(b <= a, 1.0, 0.0).astype(BF16)

        def tie_body(c, seen):
            off = pl.multiple_of(c * LANES, LANES)
            kk = keys_ref[pl.ds(off, LANES), :]
            eq = kk == thr
            ones = jnp.where(eq, 1.0, 0.0)
            rank = _dot(lower, ones.astype(BF16)) + seen
            tied = jnp.where(rank <= room, 0.0, NEG)
            tied = jnp.where(kk > INT_MIN, tied, NEG)
            madd_ref[pl.ds(off, LANES), :] = jnp.where(eq, tied, jnp.where(kk > thr, 0.0, NEG))
            return seen + jnp.sum(ones, axis=0, keepdims=True)

        lax.fori_loop(0, nc * (cw // LANES), tie_body, jnp.zeros((1, LANES), F32))


def _topk_mask_rows(keys_ref, madd_ref, nc, cw, k_top):
    rows = keys_ref.shape[0]
    sub = cw // LANES

    def count_ge(thr):
        thr_b = jnp.broadcast_to(thr, (rows, LANES))

        def body(c, acc):
            off = pl.multiple_of(c * cw, LANES)
            for s in range(sub):
                kk = keys_ref[:, pl.ds(off + s * LANES, LANES)]
                acc = acc + jnp.where(kk >= thr_b, 1, 0)
            return acc

        acc = lax.fori_loop(0, nc, body, jnp.zeros((rows, LANES), jnp.int32))
        return jnp.sum(acc, axis=1, keepdims=True)

    def bit_step(it, thr):
        cand = thr + lax.shift_left(jnp.int32(1), 31 - it)
        return jnp.where(count_ge(cand) >= k_top, cand, thr)

    thr = lax.fori_loop(0, 32, bit_step, jnp.full((rows, 1), INT_MIN, jnp.int32))
    thr_b = jnp.broadcast_to(jnp.maximum(thr, INT_MIN + 1), (rows, LANES))

    def mask_body(c, acc):
        off = pl.multiple_of(c * cw, LANES)
        for s in range(sub):
            kk = keys_ref[:, pl.ds(off + s * LANES, LANES)]
            sel = kk >= thr_b
            madd_ref[:, pl.ds(off + s * LANES, LANES)] = jnp.where(sel, 0.0, NEG)
            acc = acc + jnp.where(sel, 1, 0)
        return acc

    acc = lax.fori_loop(0, nc, mask_body, jnp.zeros((rows, LANES), jnp.int32))
    n_sel = jnp.sum(acc, axis=1, keepdims=True)

    @pl.when(jnp.max(n_sel) > k_top)
    def _():
        room = (k_top - count_ge(thr + 1)).astype(F32)
        a = lax.broadcasted_iota(jnp.int32, (LANES, LANES), 0)
        b = lax.broadcasted_iota(jnp.int32, (LANES, LANES), 1)
        upper = jnp.where(a <= b, 1.0, 0.0).astype(BF16)
        eq_thr = jnp.broadcast_to(thr, (rows, LANES))

        def tie_body(c, seen):
            off = pl.multiple_of(c * LANES, LANES)
            kk = keys_ref[:, pl.ds(off, LANES)]
            eq = kk == eq_thr
            ones = jnp.where(eq, 1.0, 0.0)
            rank = _dot(ones.astype(BF16), upper) + seen
            tied = jnp.where(rank <= room, 0.0, NEG)
            tied = jnp.where(kk > INT_MIN, tied, NEG)
            madd_ref[:, pl.ds(off, LANES)] = jnp.where(eq, tied, jnp.where(kk > eq_thr, 0.0, NEG))
            return seen + jnp.sum(ones, axis=1, keepdims=True)

        lax.fori_loop(0, nc * sub, tie_body, jnp.zeros((rows, 1), F32))


def _attn_kernel(qT_ref, qsT_ref, wiT_ref, kis_ref, ka_ref, vTa_ref, corr_ref, brow_ref, o_ref,
                 keys_ref, madd_ref, m_ref, acc_ref, s0_ref, s1_ref, cm0_ref, cm1_ref, p_ref, qs_ref, qa_ref, *, k_top):
    j = pl.program_id(1)
    t0 = j * QBLK
    cj = t0 // CHUNK
    nc = cj + 1
    qpos = t0 + lax.broadcasted_iota(jnp.int32, (1, QBLK), 1)

    for h in range(IDX_HEADS):
        qs_ref[:, h * QBLK:(h + 1) * QBLK] = jnp.concatenate([qsT_ref[0, h * LANES:(h + 1) * LANES, :]] * 2, axis=0)

    def score_body(c, carry):
        off = pl.multiple_of(c * CHUNK, CHUNK)
        sc = jnp.zeros((CHUNK, QBLK), F32)
        for half in range(IDX_HEADS // 2):
            lanes = slice(2 * half * QBLK, 2 * (half + 1) * QBLK)
            s2 = _dot(kis_ref[pl.ds(off, CHUNK), :], qs_ref[:, lanes]) * IDX_DIM ** -0.5
            for r in range(2):
                h = 2 * half + r
                sc = sc + wiT_ref[0, h:h + 1, :] * jnp.maximum(s2[:, r * QBLK:(r + 1) * QBLK], 0.0)
        kpos = off + lax.broadcasted_iota(jnp.int32, (CHUNK, 1), 0)
        keys_ref[pl.ds(off, CHUNK), :] = jnp.where(kpos <= qpos, _sortable_key(sc), INT_MIN)
        return carry

    lax.fori_loop(0, nc, score_body, 0)
    _topk_mask_cols(keys_ref, madd_ref, nc, CHUNK, k_top)

    for g in range(N_KV_HEADS):
        heads = [jnp.concatenate([qT_ref[0, h * HEAD_DIM:(h + 1) * HEAD_DIM, :], brow_ref[h].astype(BF16)], axis=0)
                 for h in range(GROUP * g, GROUP * (g + 1))]
        qa_ref[g] = jnp.concatenate(heads, axis=1)
    m_ref[...] = jnp.full(m_ref.shape, NEG, F32)
    acc_ref[...] = jnp.zeros(acc_ref.shape, F32)

    nblk = CHUNK // LANES
    s_bufs, cm_bufs = (s0_ref, s1_ref), (cm0_ref, cm1_ref)

    def logits(c, near, s_ref, cm_ref):
        off = pl.multiple_of(c * CHUNK, CHUNK)
        for g in range(N_KV_HEADS):
            mx = jnp.full((SUBLANES, GROUP * QBLK), NEG, F32)
            for i in range(nblk):
                rows = pl.ds(off + i * LANES, LANES)
                s = _dot(ka_ref[rows, g * LANES:(g + 1) * LANES], qa_ref[g])
                s = s + jnp.concatenate([madd_ref[rows, :]] * GROUP, axis=1)
                if near:
                    tile = jnp.clip(j - (c * nblk + i), 0, 2)
                    s = s + jnp.concatenate([corr_ref[tile, h] for h in range(GROUP * g, GROUP * (g + 1))], axis=1)
                s_ref[g, i * LANES:(i + 1) * LANES, :] = s
                mx = jnp.maximum(mx, jnp.max(s.reshape(LANES // SUBLANES, SUBLANES, GROUP * QBLK), axis=0))
            cm_ref[g] = jnp.max(mx, axis=0, keepdims=True)

    def accumulate(c, s_ref, cm_ref):
        off = pl.multiple_of(c * CHUNK, CHUNK)
        for g in range(N_KV_HEADS):
            m_prev = m_ref[g]
            m_next = jnp.maximum(m_prev, cm_ref[g])
            for i in range(nblk):
                blk = slice(i * LANES, (i + 1) * LANES)
                p_ref[g, blk, :] = jnp.exp2(s_ref[g, blk, :] - m_next).astype(BF16)
            pv = _dot(vTa_ref[0, g * V_AUG:(g + 1) * V_AUG, pl.ds(off, CHUNK)], p_ref[g])
            acc_ref[g] = jnp.exp2(m_prev - m_next) * acc_ref[g] + pv
            m_ref[g] = m_next

    def step(c, near):
        for par in (0, 1):
            @pl.when(c % 2 == par)
            def _(par=par):
                logits(c + 1, near, s_bufs[1 - par], cm_bufs[1 - par])
                accumulate(c, s_bufs[par], cm_bufs[par])

    def far_body(c, carry):
        step(c, False)
        return carry

    def near_body(c, carry):
        step(c, True)
        return carry

    n_far = jnp.maximum(cj - 2, 0)
    logits(0, True, s_bufs[0], cm_bufs[0])
    lax.fori_loop(0, n_far, far_body, 0)
    lax.fori_loop(n_far, cj, near_body, 0)
    for par in (0, 1):
        @pl.when(cj % 2 == par)
        def _(par=par):
            accumulate(cj, s_bufs[par], cm_bufs[par])

    for g in range(N_KV_HEADS):
        acc = acc_ref[g]
        out = acc[0:HEAD_DIM] / acc[HEAD_DIM:HEAD_DIM + 1]
        for r in range(GROUP):
            h = GROUP * g + r
            o_ref[0, h * HEAD_DIM:(h + 1) * HEAD_DIM, :] = out[:, r * QBLK:(r + 1) * QBLK].astype(o_ref.dtype)


def _ffn_kernel(x_ref, mp_ref, ga_ref, aaT_ref, wap_ref, wout_ref, ln2_ref, wgu_ref, wd_ref, lnf_ref, y_ref, *, fc):
    att = _dot_tn(aaT_ref[0], wap_ref[...])
    m = mp_ref[...] + ga_ref[...] * att
    h = x_ref[...] + _dot(m.astype(BF16), wout_ref[...])
    hn = _rms(h, ln2_ref[...]).astype(BF16)
    y = h
    for c in range(D_FF // fc):
        gate = _dot(hn, wgu_ref[:, c * fc:(c + 1) * fc])
        up = _dot(hn, wgu_ref[:, D_FF + c * fc:D_FF + (c + 1) * fc])
        act = (gate * jax.nn.sigmoid(gate)) * up
        y = y + _dot(act.astype(BF16), wd_ref[c * fc:(c + 1) * fc, :])
    y_ref[...] = _rms(y, lnf_ref[...])


def _sscore_kernel(pt_ref, qi_ref, wi_ref, *refs, pages):
    page_refs, o_ref = refs[:pages], refs[pages]
    kc = jnp.concatenate([r[0, 0] for r in page_refs], axis=1)
    s = _dot(qi_ref[0], kc, precision=lax.Precision.HIGHEST) * IDX_DIM ** -0.5
    o_ref[0] = jnp.sum(wi_ref[0] * jnp.maximum(s, 0.0), axis=0, keepdims=True)


def _sselect_kernel(sc_ref, qi_ref, kin_ref, wi_ref, madd_ref, keys_ref, *, n_past, cw, k_top):
    rows = sc_ref.shape[0]
    keys_ref[:, 0:n_past] = _sortable_key(sc_ref[...])
    kin = kin_ref[...]
    wi = wi_ref[...]
    s_new = jnp.zeros((rows, 1), F32)
    for h in range(IDX_HEADS):
        s = jnp.sum(qi_ref[:, h * IDX_DIM:(h + 1) * IDX_DIM] * kin, axis=1, keepdims=True) * IDX_DIM ** -0.5
        s_new = s_new + wi[:, h:h + 1] * jnp.maximum(s, 0.0)
    lane = lax.broadcasted_iota(jnp.int32, (rows, keys_ref.shape[1] - n_past), 1)
    keys_ref[:, n_past:] = jnp.where(lane == 0, _sortable_key(s_new), INT_MIN)
    _topk_mask_rows(keys_ref, madd_ref, keys_ref.shape[1] // cw, cw, k_top)


def _sattn_kernel(pt_ref, qbd_ref, madd_ref, maddn_ref, sb_ref, sbn_ref, kn_ref, vn_ref, *refs, pages):
    k_refs, v_refs = refs[:pages], refs[pages:2 * pages]
    o_ref, m_ref, l_ref, acc_ref = refs[2 * pages:]
    c = pl.program_id(1)

    @pl.when(c == 0)
    def _():
        m_ref[...] = jnp.full(m_ref.shape, NEG, F32)
        l_ref[...] = jnp.zeros(l_ref.shape, F32)
        acc_ref[...] = jnp.zeros(acc_ref.shape, F32)

    qbd = qbd_ref[0]
    kc = jnp.concatenate([r[0, 0].reshape(KV_WIDTH, PAGE_SIZE) for r in k_refs], axis=1).astype(BF16)
    vc = jnp.concatenate([r[0, 0].reshape(KV_WIDTH, PAGE_SIZE) for r in v_refs], axis=1).astype(BF16)
    s = _dot(qbd, kc) + sb_ref[...] + madd_ref[0]
    m_prev = m_ref[...]
    m_next = jnp.maximum(m_prev, jnp.max(s, axis=1, keepdims=True))
    alpha = jnp.exp(m_prev - m_next)
    p = jnp.exp(s - m_next)
    l_ref[...] = alpha * l_ref[...] + jnp.sum(p, axis=1, keepdims=True)
    acc_ref[...] = alpha * acc_ref[...] + _dot_nt(p.astype(BF16), vc)
    m_ref[...] = m_next

    @pl.when(c == pl.num_programs(1) - 1)
    def _():
        s_new = (jnp.sum(qbd.astype(F32) * kn_ref[0], axis=1, keepdims=True)
                 + sbn_ref[:, 0:1] + maddn_ref[0][:, 0:1])
        m_prev = m_ref[...]
        m_next = jnp.maximum(m_prev, s_new)
        alpha = jnp.exp(m_prev - m_next)
        p = jnp.exp(s_new - m_next)
        l = alpha * l_ref[...] + p
        o_ref[0] = (alpha * acc_ref[...] + p * vn_ref[0]) / l


def _full(shape):
    return pl.BlockSpec(shape, lambda *_: (0,) * len(shape))


def _pack_w_in(w):
    offs = np.cumsum((0,) + IN_SPLITS)
    wu, wq, wk, wv, wqi, wki, wwi, wg = [w[:, offs[i]:offs[i + 1]] for i in range(len(IN_SPLITS))]
    pad = jnp.zeros((D_MODEL, LANES - HEAD_DIM), w.dtype)
    wka = jnp.concatenate([piece for c in range(N_KV_HEADS) for piece in (wk[:, c * HEAD_DIM:(c + 1) * HEAD_DIM], pad)], axis=1)
    w_nat = jnp.concatenate([wu, wka, wki, wki, wg], axis=1).astype(BF16)
    wwp = jnp.pad(wwi, ((0, 0), (0, SUBLANES - IDX_HEADS)))
    w_t = jnp.concatenate([wq, wk, wv, wqi, wki, wwp], axis=1).T.astype(BF16)
    return w_nat, w_t


def _proj_prompt(x2, ln1, w_nat, w_t, pw, ps, wpp, nb, seq, tm):
    t = x2.shape[0]
    tps = seq // tm
    row = lambda n: pl.BlockSpec((tm, n), lambda i: (i, 0))
    col = lambda n: pl.BlockSpec((1, n, tm), lambda i: (i // tps, 0, i % tps))
    tshape = lambda n, dt: jax.ShapeDtypeStruct((nb, n, seq), dt)
    out_shape = (
        tshape(ATTN_WIDTH, BF16),
        tshape(KV_WIDTH, F32), tshape(KV_WIDTH, F32),
        tshape(N_KV_HEADS * V_AUG, BF16),
        tshape(IDX_DIM, F32),
        tshape(IDX_HEADS * LANES, BF16),
        tshape(SUBLANES, F32),
        jax.ShapeDtypeStruct((t, 2 * LANES), BF16),
        jax.ShapeDtypeStruct((t, N_KV_HEADS * LANES), BF16),
        jax.ShapeDtypeStruct((t, D_MODEL), F32),
        jax.ShapeDtypeStruct((t, D_MODEL), F32),
        jax.ShapeDtypeStruct((nb, 16, POOL_WIDTH), F32),
    )
    out_specs = (col(ATTN_WIDTH), col(KV_WIDTH), col(KV_WIDTH), col(N_KV_HEADS * V_AUG), col(IDX_DIM),
                 col(IDX_HEADS * LANES), col(SUBLANES), row(2 * LANES), row(N_KV_HEADS * LANES), row(D_MODEL),
                 row(D_MODEL), pl.BlockSpec((1, 16, POOL_WIDTH), lambda i: (i // tps, 0, 0)))
    return pl.pallas_call(
        functools.partial(_proj_kernel, tm=tm, tiles_per_seq=tps),
        grid=(t // tm,),
        in_specs=[row(D_MODEL), _full((1, D_MODEL)), _full(w_nat.shape), _full(w_t.shape), _full(pw.shape),
                  _full((1, POOL_WIDTH)), _full(wpp.shape)],
        out_specs=out_specs,
        out_shape=out_shape,
        scratch_shapes=[pltpu.VMEM((tm + 16, POOL_WIDTH), F32)],
        compiler_params=pltpu.CompilerParams(dimension_semantics=("arbitrary",), vmem_limit_bytes=VMEM_LIMIT),
        name="proj_pool",
    )(x2, ln1, w_nat, w_t, pw, ps, wpp)


def _proj_sample(x2, ln1, w_nat, w_t, pw, ps, wpp, st, n_past):
    m = x2.shape[0]
    shapes = [(ATTN_WIDTH, BF16), (KV_WIDTH, F32), (KV_WIDTH, F32), (IDX_DIM, F32), (IDX_HEADS * IDX_DIM, F32),
              (SUBLANES, F32), (D_MODEL, F32), (D_MODEL, F32), (POOL_WIDTH, F32)]
    return pl.pallas_call(
        functools.partial(_sproj_kernel, n_past=n_past),
        out_shape=tuple(jax.ShapeDtypeStruct((m, n), dt) for n, dt in shapes),
        compiler_params=pltpu.CompilerParams(vmem_limit_bytes=VMEM_LIMIT),
        name="proj_pool_sample",
    )(x2, ln1, w_nat, w_t, pw, ps, wpp, st)


def _bias_tables(rel_bias):
    return pl.pallas_call(
        _bias_kernel,
        in_specs=[pl.BlockSpec(memory_space=pltpu.SMEM)],
        out_shape=(jax.ShapeDtypeStruct((3, N_HEADS, QBLK, LANES), F32),
                   jax.ShapeDtypeStruct((N_HEADS, HEAD_DIM, LANES), F32),
                   jax.ShapeDtypeStruct((N_HEADS, 3 * LANES), F32)),
        name="bias_tables",
    )(rel_bias)


def _attn_prompt(qT, qsT, wiT, kis, ka, vTa, corr, brow, nb, seq, k_top):
    nq = seq // QBLK
    qcol = lambda n: pl.BlockSpec((1, n, QBLK), lambda b, j: (b, 0, j))
    seqrow = lambda n: pl.BlockSpec((seq, n), lambda b, j: (b, 0))
    return pl.pallas_call(
        functools.partial(_attn_kernel, k_top=k_top),
        grid=(nb, nq),
        in_specs=[qcol(ATTN_WIDTH), qcol(IDX_HEADS * LANES), qcol(SUBLANES), seqrow(2 * LANES), seqrow(N_KV_HEADS * LANES),
                  pl.BlockSpec((1, N_KV_HEADS * V_AUG, seq), lambda b, j: (b, 0, 0)),
                  _full(corr.shape), _full(brow.shape)],
        out_specs=qcol(ATTN_WIDTH),
        out_shape=jax.ShapeDtypeStruct((nb, ATTN_WIDTH, seq), BF16),
        scratch_shapes=[pltpu.VMEM((seq, QBLK), jnp.int32), pltpu.VMEM((seq, QBLK), F32),
                        pltpu.VMEM((N_KV_HEADS, 1, GROUP * QBLK), F32),
                        pltpu.VMEM((N_KV_HEADS, V_AUG, GROUP * QBLK), F32),
                        pltpu.VMEM((N_KV_HEADS, CHUNK, GROUP * QBLK), F32),
                        pltpu.VMEM((N_KV_HEADS, CHUNK, GROUP * QBLK), F32),
                        pltpu.VMEM((N_KV_HEADS, 1, GROUP * QBLK), F32),
                        pltpu.VMEM((N_KV_HEADS, 1, GROUP * QBLK), F32),
                        pltpu.VMEM((N_KV_HEADS, CHUNK, GROUP * QBLK), BF16),
                        pltpu.VMEM((4 * IDX_DIM, IDX_HEADS * QBLK), BF16),
                        pltpu.VMEM((N_KV_HEADS, LANES, GROUP * QBLK), BF16)],
        compiler_params=pltpu.CompilerParams(dimension_semantics=("arbitrary", "arbitrary"),
                                             vmem_limit_bytes=VMEM_LIMIT),
        name="sparse_attn",
    )(qT, qsT, wiT, kis, ka, vTa, corr, brow)


def _merge_ffn(x2, mp, ga, aaT, wap, wout, ln2, wgu, wd, lnf, seq, tm):
    t = x2.shape[0]
    tps = seq // tm
    row = lambda n: pl.BlockSpec((tm, n), lambda i: (i, 0))
    return pl.pallas_call(
        functools.partial(_ffn_kernel, fc=256),
        grid=(t // tm,),
        in_specs=[row(D_MODEL), row(D_MODEL), row(D_MODEL),
                  pl.BlockSpec((1, ATTN_WIDTH, tm), lambda i: (i // tps, 0, i % tps)),
                  _full(wap.shape), _full(wout.shape), _full((1, D_MODEL)), _full(wgu.shape), _full(wd.shape),
                  _full((1, D_MODEL))],
        out_specs=row(D_MODEL),
        out_shape=jax.ShapeDtypeStruct((t, D_MODEL), F32),
        compiler_params=pltpu.CompilerParams(dimension_semantics=("arbitrary",), vmem_limit_bytes=VMEM_LIMIT),
        name="merge_ffn",
    )(x2, mp, ga, aaT, wap, wout, ln2, wgu, wd, lnf)


def _sample_scores(page_table, qi, wi4, cache_ikT, pages):
    bd, n_pages = page_table.shape
    page_spec = lambda i: pl.BlockSpec((1, 1, IDX_DIM, PAGE_SIZE), lambda b, c, pt: (0, pt[b, c * pages + i], 0, 0))
    grid_spec = pltpu.PrefetchScalarGridSpec(
        num_scalar_prefetch=1,
        grid=(bd, n_pages // pages),
        in_specs=[pl.BlockSpec((1, IDX_HEADS, IDX_DIM), lambda b, c, pt: (b, 0, 0)),
                  pl.BlockSpec((1, IDX_HEADS, 1), lambda b, c, pt: (b, 0, 0))] + [page_spec(i) for i in range(pages)],
        out_specs=pl.BlockSpec((1, 1, pages * PAGE_SIZE), lambda b, c, pt: (b, 0, c)),
    )
    return pl.pallas_call(
        functools.partial(_sscore_kernel, pages=pages),
        grid_spec=grid_spec,
        out_shape=jax.ShapeDtypeStruct((bd, 1, n_pages * PAGE_SIZE), F32),
        compiler_params=pltpu.CompilerParams(dimension_semantics=("arbitrary", "arbitrary")),
        name="sample_scores",
    )(page_table, qi, wi4, *([cache_ikT] * pages))


def _sample_select(scores, qi, ki_new, wi, n_past, cw, k_top):
    bd = scores.shape[0]
    width = n_past + LANES
    return pl.pallas_call(
        functools.partial(_sselect_kernel, n_past=n_past, cw=cw, k_top=k_top),
        out_shape=jax.ShapeDtypeStruct((bd, width), F32),
        scratch_shapes=[pltpu.VMEM((bd, width), jnp.int32)],
        compiler_params=pltpu.CompilerParams(vmem_limit_bytes=VMEM_LIMIT),
        name="sample_select",
    )(scores, qi, ki_new, wi)


def _sample_attn(page_table, qbd, madd, madd_new, sbias, sbias_new, k_new, v_new, cache_kT, cache_vT, pages):
    bd, n_pages = page_table.shape
    chunk = pages * PAGE_SIZE
    page_spec = lambda i: pl.BlockSpec((1, 1, N_KV_HEADS, HEAD_DIM, PAGE_SIZE),
                                       lambda b, c, pt: (0, pt[b, c * pages + i], 0, 0, 0))
    per_b = lambda shape: pl.BlockSpec((1,) + shape, lambda b, c, pt: (b, 0, 0))
    grid_spec = pltpu.PrefetchScalarGridSpec(
        num_scalar_prefetch=1,
        grid=(bd, n_pages // pages),
        in_specs=[per_b((N_HEADS, KV_WIDTH)),
                  pl.BlockSpec((1, 1, chunk), lambda b, c, pt: (b, 0, c)),
                  per_b((1, LANES)),
                  pl.BlockSpec((N_HEADS, chunk), lambda b, c, pt: (0, c)),
                  pl.BlockSpec((N_HEADS, LANES), lambda b, c, pt: (0, 0)),
                  per_b((1, KV_WIDTH)), per_b((1, KV_WIDTH))]
                 + [page_spec(i) for i in range(pages)] * 2,
        out_specs=per_b((N_HEADS, KV_WIDTH)),
        scratch_shapes=[pltpu.VMEM((N_HEADS, 1), F32), pltpu.VMEM((N_HEADS, 1), F32),
                        pltpu.VMEM((N_HEADS, KV_WIDTH), F32)],
    )
    return pl.pallas_call(
        functools.partial(_sattn_kernel, pages=pages),
        grid_spec=grid_spec,
        out_shape=jax.ShapeDtypeStruct((bd, N_HEADS, KV_WIDTH), F32),
        compiler_params=pltpu.CompilerParams(dimension_semantics=("arbitrary", "arbitrary"),
                                             vmem_limit_bytes=VMEM_LIMIT),
        name="sample_attn",
    )(page_table, qbd, madd, madd_new, sbias, sbias_new, k_new, v_new, *([cache_kT] * pages), *([cache_vT] * pages))


def _pick(n, prefs):
    for p in prefs:
        if n % p == 0:
            return p
    return n


def kernel(x_prompt, x_sample, cache_k, cache_v, cache_idx_k, state_pool, page_table, ln1, w_in, pool_w, pool_scale,
           w_pool_proj, w_attn_proj, w_out, ln2, w_gate_up, w_down, rel_bias, ln_final):
    assert w_in.shape[0] == 1, "one layer"
    nb, seq, _ = x_prompt.shape
    bd, dec_seq, _ = x_sample.shape
    assert dec_seq == 1 and seq % CHUNK == 0
    n_pages = page_table.shape[1]
    n_past = n_pages * PAGE_SIZE

    w_nat, w_t = _pack_w_in(w_in[0])
    ln1r, ln2r, lnfr = ln1[0][None], ln2[0][None], ln_final[None]
    pw = pool_w[0].astype(BF16)
    ps = pool_scale[0][None]
    wpp, wap, wout = w_pool_proj[0].astype(BF16), w_attn_proj[0].astype(BF16), w_out[0].astype(BF16)
    wgu, wd = w_gate_up[0].astype(BF16), w_down[0].astype(BF16)
    corr, brow, srow = _bias_tables(rel_bias)

    xp = x_prompt.reshape(nb * seq, D_MODEL)
    qT, kT, vT, vTa, kiT, qsT, wiT, kis, ka, mp, ga, ul = _proj_prompt(
        xp, ln1r, w_nat, w_t, pw, ps, wpp, nb, seq, _pick(seq, (512, 256, 128)))
    aaT = _attn_prompt(qT, qsT, wiT, kis, ka, vTa, corr, brow, nb, seq, min(TOPK_MAX, seq // 4))
    y_prompt = _merge_ffn(xp, mp, ga, aaT, wap, wout, ln2r, wgu, wd, lnfr, seq, _pick(seq, (256, 128)))
    y_prompt = y_prompt.reshape(nb, seq, D_MODEL)

    xs = x_sample.reshape(bd, D_MODEL)
    st = jnp.swapaxes(state_pool[0], 0, 1)
    qs, ks, vs, kin, qin, wis, mps, gas, us = _proj_sample(xs, ln1r, w_nat, w_t, pw, ps, wpp, st, n_past)
    pages = _pick(n_pages, (16, 8, 4, 2))
    scores = _sample_scores(page_table, qin.reshape(bd, IDX_HEADS, IDX_DIM), wis[:, :IDX_HEADS, None],
                            jnp.swapaxes(cache_idx_k, 2, 3), _pick(n_pages, (32, 16, 8, 4, 2))).reshape(bd, n_past)
    width = n_past + LANES
    madd = _sample_select(scores, qin, kin, wis, n_past, _pick(width, (640, 512, 384, 256, 128)),
                          min(TOPK_MAX, (n_past + 1) // 4))
    head_kv = np.arange(N_HEADS) // GROUP
    onehot = jnp.asarray(head_kv[:, None] == np.arange(N_KV_HEADS)[None, :])
    qbd = jnp.where(onehot[None, :, :, None], qs.reshape(bd, N_HEADS, 1, HEAD_DIM), 0).reshape(bd, N_HEADS, KV_WIDTH)
    far = jnp.broadcast_to(srow[:, 2 * LANES:2 * LANES + 1], (N_HEADS, n_past - LANES))
    sbias = jnp.concatenate([far, srow[:, :LANES]], axis=1)
    ao = _sample_attn(page_table, qbd, madd[:, None, :n_past], madd[:, None, n_past:], sbias, srow[:, LANES:2 * LANES],
                      ks[:, None, :], vs[:, None, :], jnp.transpose(cache_k, (0, 1, 3, 4, 2)),
                      jnp.transpose(cache_v, (0, 1, 3, 4, 2)), pages)
    aas = jnp.sum(jnp.where(onehot[None, :, :, None], ao.reshape(bd, N_HEADS, N_KV_HEADS, HEAD_DIM), 0), axis=2)
    aasT = aas.reshape(bd, ATTN_WIDTH).astype(BF16).T[None]
    y_sample = _merge_ffn(xs, mps, gas, aasT, wap, wout, ln2r, wgu, wd, lnfr, bd, bd).reshape(bd, 1, D_MODEL)

    to_heads = lambda a: jnp.transpose(a.reshape(nb, N_KV_HEADS, HEAD_DIM, seq), (0, 3, 1, 2))[None]
    return (y_prompt, y_sample, to_heads(kT), to_heads(vT), jnp.swapaxes(kiT, 1, 2)[None], ul[None, :, 1:, :],
            ks.reshape(1, bd, 1, N_KV_HEADS, HEAD_DIM), vs.reshape(1, bd, 1, N_KV_HEADS, HEAD_DIM),
            kin.reshape(1, bd, 1, IDX_DIM),
            jnp.concatenate([state_pool[0][:, 1:], us[:, None, :]], axis=1)[None])
```

```python
import functools
import math

import jax
import jax.numpy as jnp
import numpy as np
from jax import lax
from jax.experimental import pallas as pl
from jax.experimental.pallas import tpu as pltpu

D_MODEL = 1024
PAGE_SIZE = 128
POOL_WIDTH = D_MODEL // 2
POOL_WINDOWS = (2, 4, 8, 16)
POOL_GROUP_DIM = POOL_WIDTH // len(POOL_WINDOWS)
POOL_BUF = 15
N_HEADS = 8
N_KV_HEADS = 4
HEAD_DIM = 64
GROUP = N_HEADS // N_KV_HEADS
ATTN_WIDTH = N_HEADS * HEAD_DIM
KV_WIDTH = N_KV_HEADS * HEAD_DIM
IDX_HEADS = 4
IDX_DIM = 64
TOPK_MAX = 256
N_BUCKETS = 32
MAX_DISTANCE = 128
D_FF = -(-(8 * D_MODEL) // (3 * 256)) * 256
EPS = 1e-6
IN_SPLITS = (POOL_WIDTH, ATTN_WIDTH, KV_WIDTH, KV_WIDTH, IDX_HEADS * IDX_DIM, IDX_DIM, IDX_HEADS, 2 * D_MODEL)

LANES = 128
SUBLANES = 8
QBLK = 128
CHUNK = 512
INT_MIN = -(2 ** 31)
NEG = -1e30
LOG2E = math.log2(math.e)
VMEM_LIMIT = 56 * 1024 * 1024
V_AUG = HEAD_DIM + 16

N_U, N_KA, N_KK, N_G = 0, 512, 1024, 1152
N_END = N_G + 2 * D_MODEL
T_Q, T_K, T_V, T_QI, T_KI, T_WI = 0, 512, 768, 1024, 1280, 1344
T_END = T_WI + SUBLANES
QP_ROWS = ATTN_WIDTH + IDX_HEADS * LANES

F32 = jnp.float32
BF16 = jnp.bfloat16


def _dot(a, b, precision=None):
    return jnp.dot(a, b, preferred_element_type=F32, precision=precision)


def _dot_nt(a, b):
    return lax.dot_general(a, b, (((1,), (1,)), ((), ())), preferred_element_type=F32)


def _dot_tn(a, b):
    return lax.dot_general(a, b, (((0,), (0,)), ((), ())), preferred_element_type=F32)


def _rms(x, g):
    return (x * lax.rsqrt(jnp.mean(x * x, axis=-1, keepdims=True) + EPS)) * g


def _hi_lo(z):
    hi = z.astype(BF16).astype(F32)
    return hi, z - hi


def _sortable_key(score):
    bits = lax.bitcast_convert_type(score, jnp.int32)
    return jnp.where(bits < 0, -(bits & 0x7FFFFFFF), bits)


def _gated_pool(d_groups, gates, pw_ref, ps_ref, wpp_ref):
    a = [_dot(d.astype(BF16), pw_ref[g]) for g, d in enumerate(d_groups)]
    a = jnp.concatenate(a, axis=1) * ps_ref[...]
    ap = _dot(a.astype(BF16), wpp_ref[...])
    return jax.nn.sigmoid(gates[:, :D_MODEL]) * ap


def _proj_kernel(x_ref, ln1_ref, wn_ref, wt_ref, pw_ref, ps_ref, wpp_ref,
                 qp_ref, kT_ref, vT_ref, vTa_ref, kiT_ref, wiT_ref, kis_ref, ka_ref, mp_ref, ga_ref, ul_ref,
                 uext_ref, *, tm, tiles_per_seq):
    ti = pl.program_id(0) % tiles_per_seq
    hb = _rms(x_ref[...], ln1_ref[...]).astype(BF16)

    def mm(lo, hi):
        return _dot(hb, wn_ref[:, lo:hi])

    def mt(lo, hi):
        return _dot_nt(wt_ref[lo:hi, :], hb)

    def per_qblk(ref, row0, val):
        for i in range(tm // QBLK):
            ref[0, i, row0:row0 + val.shape[0], :] = val[:, i * QBLK:(i + 1) * QBLK]

    per_qblk(qp_ref, 0, (mt(T_Q, T_K) * (HEAD_DIM ** -0.5 * LOG2E)).astype(BF16))
    kT_ref[0] = mt(T_K, T_V)
    vT = mt(T_V, T_QI)
    vT_ref[0] = vT
    ones = jnp.ones((V_AUG - HEAD_DIM, tm), F32)
    vTa_ref[0] = jnp.concatenate(
        [piece for c in range(N_KV_HEADS) for piece in (vT[c * HEAD_DIM:(c + 1) * HEAD_DIM], ones)], axis=0).astype(BF16)
    hi, lo = _hi_lo(mt(T_QI, T_KI))
    per_qblk(qp_ref, ATTN_WIDTH, jnp.concatenate(
        [piece for h in range(IDX_HEADS) for piece in (hi[h * IDX_DIM:(h + 1) * IDX_DIM], lo[h * IDX_DIM:(h + 1) * IDX_DIM])],
        axis=0).astype(BF16))
    kiT_ref[0] = mt(T_KI, T_WI)
    per_qblk(wiT_ref, 0, mt(T_WI, T_END) * IDX_HEADS ** -0.5)

    hi, lo = _hi_lo(mm(N_KK, N_G))
    kis_ref[...] = jnp.concatenate([hi, lo], axis=1).astype(BF16)
    lane = lax.broadcasted_iota(jnp.int32, (1, N_KV_HEADS * LANES), 1) % LANES
    bias_cols = jnp.where((lane == HEAD_DIM) | (lane == HEAD_DIM + 1), 1.0, 0.0)
    ka_ref[...] = (mm(N_KA, N_KK) + bias_cols).astype(BF16)

    u = mm(N_U, N_KA)

    @pl.when(ti == 0)
    def _():
        uext_ref[0:16, :] = jnp.zeros((16, POOL_WIDTH), F32)

    @pl.when(ti != 0)
    def _():
        uext_ref[0:16, :] = uext_ref[tm:tm + 16, :]

    uext_ref[16:16 + tm, :] = u
    ul_ref[0] = uext_ref[tm:tm + 16, :]
    pos1 = ti * tm + lax.broadcasted_iota(jnp.int32, (tm, 1), 0) + 1
    ds = []
    for g, w in enumerate(POOL_WINDOWS):
        sl = slice(g * POOL_GROUP_DIM, (g + 1) * POOL_GROUP_DIM)
        win = uext_ref[16:16 + tm, sl]
        for d in range(1, w):
            win = win + uext_ref[16 - d:16 - d + tm, sl]
        cnt = jnp.minimum(w, pos1).astype(F32)
        ds.append(win / cnt - u[:, sl])
    gates = mm(N_G, N_END)
    mp_ref[...] = _gated_pool(ds, gates, pw_ref, ps_ref, wpp_ref)
    ga_ref[...] = jax.nn.sigmoid(gates[:, D_MODEL:])


def _sproj_kernel(x_ref, ln1_ref, wn_ref, wt_ref, pw_ref, ps_ref, wpp_ref, st_ref,
                  q_ref, k_ref, v_ref, ki_ref, qi_ref, wi_ref, mp_ref, ga_ref, u_ref, *, n_past):
    hb = _rms(x_ref[...], ln1_ref[...]).astype(BF16)

    def mm(lo, hi):
        return _dot(hb, wn_ref[:, lo:hi])

    def mn(lo, hi):
        return _dot_nt(hb, wt_ref[lo:hi, :])

    q_ref[...] = (mn(T_Q, T_K) * HEAD_DIM ** -0.5).astype(BF16)
    k_ref[...] = mn(T_K, T_V)
    v_ref[...] = mn(T_V, T_QI)
    qi_ref[...] = mn(T_QI, T_KI)
    ki_ref[...] = mn(T_KI, T_WI)
    wi_ref[...] = mn(T_WI, T_END) * IDX_HEADS ** -0.5
    u = mm(N_U, N_KA)
    u_ref[...] = u
    ds = []
    for g, w in enumerate(POOL_WINDOWS):
        sl = slice(g * POOL_GROUP_DIM, (g + 1) * POOL_GROUP_DIM)
        win = u[:, sl]
        for d in range(1, w):
            win = win + st_ref[POOL_BUF - d][:, sl]
        ds.append(win / float(min(w, n_past + 1)) - u[:, sl])
    gates = mm(N_G, N_END)
    mp_ref[...] = _gated_pool(ds, gates, pw_ref, ps_ref, wpp_ref)
    ga_ref[...] = jax.nn.sigmoid(gates[:, D_MODEL:])


def _bias_kernel(rb_ref, corr_ref, brow_ref, srow_ref):
    i = lax.broadcasted_iota(jnp.int32, (QBLK, LANES), 0)
    j = lax.broadcasted_iota(jnp.int32, (QBLK, LANES), 1)
    max_exact = N_BUCKETS // 2

    def bias_of(dist, h):
        d = jnp.maximum(dist, 1).astype(F32)
        large = max_exact + (jnp.log(d / max_exact) / math.log(MAX_DISTANCE / max_exact)
                             * (N_BUCKETS - max_exact)).astype(jnp.int32)
        bucket = jnp.where(dist < max_exact, dist, jnp.minimum(large, N_BUCKETS - 1))
        acc = jnp.zeros(dist.shape, F32)
        for b in range(N_BUCKETS):
            acc = jnp.where(bucket == b, rb_ref[b, h], acc)
        return acc

    row = lax.broadcasted_iota(jnp.int32, (HEAD_DIM, LANES), 0)
    lane1 = lax.broadcasted_iota(jnp.int32, (1, LANES), 1)
    for h in range(N_HEADS):
        far = rb_ref[N_BUCKETS - 1, h]
        for t in range(2):
            corr_ref[t, h] = (bias_of(jnp.maximum(t * LANES + j - i, 0), h) - far) * LOG2E
        corr_ref[2, h] = jnp.zeros((QBLK, LANES), F32)
        far_v = jnp.full((HEAD_DIM, LANES), far, F32) * LOG2E
        hi, lo = _hi_lo(far_v)
        brow_ref[h] = jnp.where(row == 0, hi, jnp.where(row == 1, lo, 0.0))
        srow_ref[h:h + 1, :] = jnp.concatenate([bias_of(LANES - lane1, h), bias_of(jnp.zeros((1, LANES), jnp.int32), h),
                                       jnp.full((1, LANES), far, F32)], axis=1)


def _topk_mask_cols(keys_ref, madd_ref, nc, cw, k_top):
    part = 4 * SUBLANES

    def count_ge(thr):
        def body(c, acc):
            kk = keys_ref[pl.ds(pl.multiple_of(c * cw, cw), cw), :]
            for i in range(cw // part):
                acc = jnp.where(kk[i * part:(i + 1) * part] >= thr, acc + 1, acc)
            return acc

        acc = lax.fori_loop(0, nc, body, jnp.zeros((part, LANES), jnp.int32))
        return jnp.sum(acc, axis=0, keepdims=True)

    def bit_step(it, thr):
        cand = thr + lax.shift_left(jnp.int32(1), 31 - it)
        return jnp.where(count_ge(cand) >= k_top, cand, thr)

    thr = lax.fori_loop(0, 32, bit_step, jnp.full((1, LANES), INT_MIN, jnp.int32))
    thr_eff = jnp.maximum(thr, INT_MIN + 1)

    def mask_body(c, acc):
        off = pl.multiple_of(c * cw, cw)
        sel = keys_ref[pl.ds(off, cw), :] >= thr_eff
        madd_ref[pl.ds(off, cw), :] = jnp.where(sel, 0.0, NEG)
        return acc + jnp.sum(jnp.where(sel, 1, 0).reshape(cw // part, part, LANES), axis=0)

    acc = lax.fori_loop(0, nc, mask_body, jnp.zeros((part, LANES), jnp.int32))
    n_sel = jnp.sum(acc, axis=0, keepdims=True)

    @pl.when(jnp.max(n_sel) > k_top)
    def _():
        room = (k_top - count_ge(thr + 1)).astype(F32)
        a = lax.broadcasted_iota(jnp.int32, (LANES, LANES), 0)
        b = lax.broadcasted_iota(jnp.int32, (LANES, LANES), 1)
        lower = jnp.where(b <= a, 1.0, 0.0).astype(BF16)

        def tie_body(c, seen):
            off = pl.multiple_of(c * LANES, LANES)
            kk = keys_ref[pl.ds(off, LANES), :]
            eq = kk == thr
            ones = jnp.where(eq, 1.0, 0.0)
            rank = _dot(lower, ones.astype(BF16)) + seen
            tied = jnp.where(rank <= room, 0.0, NEG)
            tied = jnp.where(kk > INT_MIN, tied, NEG)
            madd_ref[pl.ds(off, LANES), :] = jnp.where(eq, tied, jnp.where(kk > thr, 0.0, NEG))
            return seen + jnp.sum(ones, axis=0, keepdims=True)

        lax.fori_loop(0, nc * (cw // LANES), tie_body, jnp.zeros((1, LANES), F32))


def _topk_mask_rows(keys_ref, madd_ref, nc, cw, k_top):
    rows = keys_ref.shape[0]
    sub = cw // LANES

    def count_ge(thr):
        thr_b = jnp.broadcast_to(thr, (rows, LANES))

        def body(c, acc):
            off = pl.multiple_of(c * cw, LANES)
            for s in range(sub):
                kk = keys_ref[:, pl.ds(off + s * LANES, LANES)]
                acc = acc + jnp.where(kk >= thr_b, 1, 0)
            return acc

        acc = lax.fori_loop(0, nc, body, jnp.zeros((rows, LANES), jnp.int32))
        return jnp.sum(acc, axis=1, keepdims=True)

    def bit_step(it, thr):
        cand = thr + lax.shift_left(jnp.int32(1), 31 - it)
        return jnp.where(count_ge(cand) >= k_top, cand, thr)

    thr = lax.fori_loop(0, 32, bit_step, jnp.full((rows, 1), INT_MIN, jnp.int32))
    thr_b = jnp.broadcast_to(jnp.maximum(thr, INT_MIN + 1), (rows, LANES))

    def mask_body(c, acc):
        off = pl.multiple_of(c * cw, LANES)
        for s in range(sub):
            kk = keys_ref[:, pl.ds(off + s * LANES, LANES)]
            sel = kk >= thr_b
            madd_ref[:, pl.ds(off + s * LANES, LANES)] = jnp.where(sel, 0.0, NEG)
            acc = acc + jnp.where(sel, 1, 0)
        return acc

    acc = lax.fori_loop(0, nc, mask_body, jnp.zeros((rows, LANES), jnp.int32))
    n_sel = jnp.sum(acc, axis=1, keepdims=True)

    @pl.when(jnp.max(n_sel) > k_top)
    def _():
        room = (k_top - count_ge(thr + 1)).astype(F32)
        a = lax.broadcasted_iota(jnp.int32, (LANES, LANES), 0)
        b = lax.broadcasted_iota(jnp.int32, (LANES, LANES), 1)
        upper = jnp.where(a <= b, 1.0, 0.0).astype(BF16)
        eq_thr = jnp.broadcast_to(thr, (rows, LANES))

        def tie_body(c, seen):
            off = pl.multiple_of(c * LANES, LANES)
            kk = keys_ref[:, pl.ds(off, LANES)]
            eq = kk == eq_thr
            ones = jnp.where(eq, 1.0, 0.0)
            rank = _dot(ones.astype(BF16), upper) + seen
            tied = jnp.where(rank <= room, 0.0, NEG)
            tied = jnp.where(kk > INT_MIN, tied, NEG)
            madd_ref[:, pl.ds(off, LANES)] = jnp.where(eq, tied, jnp.where(kk > eq_thr, 0.0, NEG))
            return seen + jnp.sum(ones, axis=1, keepdims=True)

        lax.fori_loop(0, nc * sub, tie_body, jnp.zeros((rows, 1), F32))


def _attn_kernel(qp_ref, wiT_ref, kis_ref, ka_ref, vTa_ref, corr_ref, brow_ref, o_ref,
                 keys_ref, madd_ref, m_ref, acc_ref, s0_ref, s1_ref, cm0_ref, cm1_ref, p_ref, qs_ref, qa_ref, *, k_top):
    j = pl.program_id(1)
    t0 = j * QBLK
    cj = t0 // CHUNK
    nc = cj + 1
    qpos = t0 + lax.broadcasted_iota(jnp.int32, (1, QBLK), 1)

    for h in range(IDX_HEADS):
        rows = slice(ATTN_WIDTH + h * LANES, ATTN_WIDTH + (h + 1) * LANES)
        qs_ref[:, h * QBLK:(h + 1) * QBLK] = jnp.concatenate([qp_ref[0, 0, rows, :]] * 2, axis=0)

    def score_body(c, carry):
        off = pl.multiple_of(c * CHUNK, CHUNK)
        sc = jnp.zeros((CHUNK, QBLK), F32)
        for half in range(IDX_HEADS // 2):
            lanes = slice(2 * half * QBLK, 2 * (half + 1) * QBLK)
            s2 = _dot(kis_ref[pl.ds(off, CHUNK), :], qs_ref[:, lanes]) * IDX_DIM ** -0.5
            for r in range(2):
                h = 2 * half + r
                sc = sc + wiT_ref[0, 0, h:h + 1, :] * jnp.maximum(s2[:, r * QBLK:(r + 1) * QBLK], 0.0)
        kpos = off + lax.broadcasted_iota(jnp.int32, (CHUNK, 1), 0)
        keys_ref[pl.ds(off, CHUNK), :] = jnp.where(kpos <= qpos, _sortable_key(sc), INT_MIN)
        return carry

    lax.fori_loop(0, nc, score_body, 0)
    _topk_mask_cols(keys_ref, madd_ref, nc, CHUNK, k_top)

    for g in range(N_KV_HEADS):
        heads = [jnp.concatenate([qp_ref[0, 0, h * HEAD_DIM:(h + 1) * HEAD_DIM, :], brow_ref[h].astype(BF16)], axis=0)
                 for h in range(GROUP * g, GROUP * (g + 1))]
        qa_ref[g] = jnp.concatenate(heads, axis=1)
    m_ref[...] = jnp.full(m_ref.shape, NEG, F32)
    acc_ref[...] = jnp.zeros(acc_ref.shape, F32)

    nblk = CHUNK // LANES
    s_bufs, cm_bufs = (s0_ref, s1_ref), (cm0_ref, cm1_ref)

    def logits(c, near, s_ref, cm_ref):
        off = pl.multiple_of(c * CHUNK, CHUNK)
        for g in range(N_KV_HEADS):
            mx = jnp.full((SUBLANES, GROUP * QBLK), NEG, F32)
            for i in range(nblk):
                rows = pl.ds(off + i * LANES, LANES)
                s = _dot(ka_ref[rows, g * LANES:(g + 1) * LANES], qa_ref[g])
                s = s + jnp.concatenate([madd_ref[rows, :]] * GROUP, axis=1)
                if near:
                    tile = jnp.clip(j - (c * nblk + i), 0, 2)
                    s = s + jnp.concatenate([corr_ref[tile, h] for h in range(GROUP * g, GROUP * (g + 1))], axis=1)
                s_ref[g, i * LANES:(i + 1) * LANES, :] = s
                mx = jnp.maximum(mx, jnp.max(s.reshape(LANES // SUBLANES, SUBLANES, GROUP * QBLK), axis=0))
            cm_ref[g] = jnp.max(mx, axis=0, keepdims=True)

    def accumulate(c, s_ref, cm_ref):
        off = pl.multiple_of(c * CHUNK, CHUNK)
        for g in range(N_KV_HEADS):
            m_prev = m_ref[g]
            m_next = jnp.maximum(m_prev, cm_ref[g])
            for i in range(nblk):
                blk = slice(i * LANES, (i + 1) * LANES)
                p_ref[g, blk, :] = jnp.exp2(s_ref[g, blk, :] - m_next).astype(BF16)
            pv = _dot(vTa_ref[0, g * V_AUG:(g + 1) * V_AUG, pl.ds(off, CHUNK)], p_ref[g])
            acc_ref[g] = jnp.exp2(m_prev - m_next) * acc_ref[g] + pv
            m_ref[g] = m_next

    def step(c, near):
        for par in (0, 1):
            @pl.when(c % 2 == par)
            def _(par=par):
                logits(c + 1, near, s_bufs[1 - par], cm_bufs[1 - par])
                accumulate(c, s_bufs[par], cm_bufs[par])

    def far_body(c, carry):
        step(c, False)
        return carry

    def near_body(c, carry):
        step(c, True)
        return carry

    n_far = jnp.maximum(cj - 2, 0)
    logits(0, True, s_bufs[0], cm_bufs[0])
    lax.fori_loop(0, n_far, far_body, 0)
    lax.fori_loop(n_far, cj, near_body, 0)
    for par in (0, 1):
        @pl.when(cj % 2 == par)
        def _(par=par):
            accumulate(cj, s_bufs[par], cm_bufs[par])

    for g in range(N_KV_HEADS):
        acc = acc_ref[g]
        out = acc[0:HEAD_DIM] / acc[HEAD_DIM:HEAD_DIM + 1]
        for r in range(GROUP):
            h = GROUP * g + r
            o_ref[0, 0, h * HEAD_DIM:(h + 1) * HEAD_DIM, :] = out[:, r * QBLK:(r + 1) * QBLK].astype(o_ref.dtype)


def _ffn_kernel(x_ref, mp_ref, ga_ref, aaT_ref, wap_ref, wout_ref, ln2_ref, wgu_ref, wd_ref, lnf_ref, y_ref, *, fc):
    aaT = jnp.concatenate([aaT_ref[0, i] for i in range(aaT_ref.shape[1])], axis=1)
    att = _dot_tn(aaT, wap_ref[...])
    m = mp_ref[...] + ga_ref[...] * att
    h = x_ref[...] + _dot(m.astype(BF16), wout_ref[...])
    hn = _rms(h, ln2_ref[...]).astype(BF16)
    y = h
    for c in range(D_FF // fc):
        gate = _dot(hn, wgu_ref[:, c * fc:(c + 1) * fc])
        up = _dot(hn, wgu_ref[:, D_FF + c * fc:D_FF + (c + 1) * fc])
        act = (gate * jax.nn.sigmoid(gate)) * up
        y = y + _dot(act.astype(BF16), wd_ref[c * fc:(c + 1) * fc, :])
    y_ref[...] = _rms(y, lnf_ref[...])


def _sscore_kernel(pt_ref, qi_ref, wi_ref, *refs, pages):
    page_refs, o_ref = refs[:pages], refs[pages]
    kc = jnp.concatenate([r[0, 0] for r in page_refs], axis=1)
    s = _dot(qi_ref[0], kc, precision=lax.Precision.HIGHEST) * IDX_DIM ** -0.5
    o_ref[0] = jnp.sum(wi_ref[0] * jnp.maximum(s, 0.0), axis=0, keepdims=True)


def _sselect_kernel(sc_ref, qi_ref, kin_ref, wi_ref, madd_ref, keys_ref, *, n_past, cw, k_top):
    rows = sc_ref.shape[0]
    keys_ref[:, 0:n_past] = _sortable_key(sc_ref[...])
    kin = kin_ref[...]
    wi = wi_ref[...]
    s_new = jnp.zeros((rows, 1), F32)
    for h in range(IDX_HEADS):
        s = jnp.sum(qi_ref[:, h * IDX_DIM:(h + 1) * IDX_DIM] * kin, axis=1, keepdims=True) * IDX_DIM ** -0.5
        s_new = s_new + wi[:, h:h + 1] * jnp.maximum(s, 0.0)
    lane = lax.broadcasted_iota(jnp.int32, (rows, keys_ref.shape[1] - n_past), 1)
    keys_ref[:, n_past:] = jnp.where(lane == 0, _sortable_key(s_new), INT_MIN)
    _topk_mask_rows(keys_ref, madd_ref, keys_ref.shape[1] // cw, cw, k_top)


def _sattn_kernel(pt_ref, qbd_ref, madd_ref, maddn_ref, sb_ref, sbn_ref, kn_ref, vn_ref, *refs, pages):
    k_refs, v_refs = refs[:pages], refs[pages:2 * pages]
    o_ref, m_ref, l_ref, acc_ref = refs[2 * pages:]
    c = pl.program_id(1)

    @pl.when(c == 0)
    def _():
        m_ref[...] = jnp.full(m_ref.shape, NEG, F32)
        l_ref[...] = jnp.zeros(l_ref.shape, F32)
        acc_ref[...] = jnp.zeros(acc_ref.shape, F32)

    qbd = qbd_ref[0]
    kc = jnp.concatenate([r[0, 0].reshape(KV_WIDTH, PAGE_SIZE) for r in k_refs], axis=1).astype(BF16)
    vc = jnp.concatenate([r[0, 0].reshape(KV_WIDTH, PAGE_SIZE) for r in v_refs], axis=1).astype(BF16)
    s = _dot(qbd, kc) + sb_ref[...] + madd_ref[0]
    m_prev = m_ref[...]
    m_next = jnp.maximum(m_prev, jnp.max(s, axis=1, keepdims=True))
    alpha = jnp.exp(m_prev - m_next)
    p = jnp.exp(s - m_next)
    l_ref[...] = alpha * l_ref[...] + jnp.sum(p, axis=1, keepdims=True)
    acc_ref[...] = alpha * acc_ref[...] + _dot_nt(p.astype(BF16), vc)
    m_ref[...] = m_next

    @pl.when(c == pl.num_programs(1) - 1)
    def _():
        s_new = (jnp.sum(qbd.astype(F32) * kn_ref[0], axis=1, keepdims=True)
                 + sbn_ref[:, 0:1] + maddn_ref[0][:, 0:1])
        m_prev = m_ref[...]
        m_next = jnp.maximum(m_prev, s_new)
        alpha = jnp.exp(m_prev - m_next)
        p = jnp.exp(s_new - m_next)
        l = alpha * l_ref[...] + p
        o_ref[0] = (alpha * acc_ref[...] + p * vn_ref[0]) / l


def _full(shape):
    return pl.BlockSpec(shape, lambda *_: (0,) * len(shape))


def _pack_w_in(w):
    offs = np.cumsum((0,) + IN_SPLITS)
    wu, wq, wk, wv, wqi, wki, wwi, wg = [w[:, offs[i]:offs[i + 1]] for i in range(len(IN_SPLITS))]
    pad = jnp.zeros((D_MODEL, LANES - HEAD_DIM), w.dtype)
    wka = jnp.concatenate([piece for c in range(N_KV_HEADS) for piece in (wk[:, c * HEAD_DIM:(c + 1) * HEAD_DIM], pad)], axis=1)
    w_nat = jnp.concatenate([wu, wka, wki, wki, wg], axis=1).astype(BF16)
    wwp = jnp.pad(wwi, ((0, 0), (0, SUBLANES - IDX_HEADS)))
    w_t = jnp.concatenate([wq, wk, wv, wqi, wki, wwp], axis=1).T.astype(BF16)
    return w_nat, w_t


def _proj_prompt(x2, ln1, w_nat, w_t, pw, ps, wpp, nb, seq, tm):
    t = x2.shape[0]
    tps = seq // tm
    row = lambda n: pl.BlockSpec((tm, n), lambda i: (i, 0))
    col = lambda n: pl.BlockSpec((1, n, tm), lambda i: (i // tps, 0, i % tps))
    tshape = lambda n, dt: jax.ShapeDtypeStruct((nb, n, seq), dt)
    qblk = lambda n: pl.BlockSpec((1, tm // QBLK, n, QBLK), lambda i: (i // tps, i % tps, 0, 0))
    out_shape = (
        jax.ShapeDtypeStruct((nb, seq // QBLK, QP_ROWS, QBLK), BF16),
        tshape(KV_WIDTH, F32), tshape(KV_WIDTH, F32),
        tshape(N_KV_HEADS * V_AUG, BF16),
        tshape(IDX_DIM, F32),
        jax.ShapeDtypeStruct((nb, seq // QBLK, SUBLANES, QBLK), F32),
        jax.ShapeDtypeStruct((t, 2 * LANES), BF16),
        jax.ShapeDtypeStruct((t, N_KV_HEADS * LANES), BF16),
        jax.ShapeDtypeStruct((t, D_MODEL), F32),
        jax.ShapeDtypeStruct((t, D_MODEL), F32),
        jax.ShapeDtypeStruct((nb, 16, POOL_WIDTH), F32),
    )
    out_specs = (qblk(QP_ROWS), col(KV_WIDTH), col(KV_WIDTH), col(N_KV_HEADS * V_AUG), col(IDX_DIM),
                 qblk(SUBLANES), row(2 * LANES), row(N_KV_HEADS * LANES), row(D_MODEL),
                 row(D_MODEL), pl.BlockSpec((1, 16, POOL_WIDTH), lambda i: (i // tps, 0, 0)))
    return pl.pallas_call(
        functools.partial(_proj_kernel, tm=tm, tiles_per_seq=tps),
        grid=(t // tm,),
        in_specs=[row(D_MODEL), _full((1, D_MODEL)), _full(w_nat.shape), _full(w_t.shape), _full(pw.shape),
                  _full((1, POOL_WIDTH)), _full(wpp.shape)],
        out_specs=out_specs,
        out_shape=out_shape,
        scratch_shapes=[pltpu.VMEM((tm + 16, POOL_WIDTH), F32)],
        compiler_params=pltpu.CompilerParams(dimension_semantics=("arbitrary",), vmem_limit_bytes=VMEM_LIMIT),
        name="proj_pool",
    )(x2, ln1, w_nat, w_t, pw, ps, wpp)


def _proj_sample(x2, ln1, w_nat, w_t, pw, ps, wpp, st, n_past):
    m = x2.shape[0]
    shapes = [(ATTN_WIDTH, BF16), (KV_WIDTH, F32), (KV_WIDTH, F32), (IDX_DIM, F32), (IDX_HEADS * IDX_DIM, F32),
              (SUBLANES, F32), (D_MODEL, F32), (D_MODEL, F32), (POOL_WIDTH, F32)]
    return pl.pallas_call(
        functools.partial(_sproj_kernel, n_past=n_past),
        out_shape=tuple(jax.ShapeDtypeStruct((m, n), dt) for n, dt in shapes),
        compiler_params=pltpu.CompilerParams(vmem_limit_bytes=VMEM_LIMIT),
        name="proj_pool_sample",
    )(x2, ln1, w_nat, w_t, pw, ps, wpp, st)


def _bias_tables(rel_bias):
    return pl.pallas_call(
        _bias_kernel,
        in_specs=[pl.BlockSpec(memory_space=pltpu.SMEM)],
        out_shape=(jax.ShapeDtypeStruct((3, N_HEADS, QBLK, LANES), F32),
                   jax.ShapeDtypeStruct((N_HEADS, HEAD_DIM, LANES), F32),
                   jax.ShapeDtypeStruct((N_HEADS, 3 * LANES), F32)),
        name="bias_tables",
    )(rel_bias)


def _attn_prompt(qp, wiT, kis, ka, vTa, corr, brow, nb, seq, k_top):
    nq = seq // QBLK
    qblk = lambda n: pl.BlockSpec((1, 1, n, QBLK), lambda b, j: (b, j, 0, 0))
    seqrow = lambda n: pl.BlockSpec((seq, n), lambda b, j: (b, 0))
    return pl.pallas_call(
        functools.partial(_attn_kernel, k_top=k_top),
        grid=(nb, nq),
        in_specs=[qblk(QP_ROWS), qblk(SUBLANES), seqrow(2 * LANES), seqrow(N_KV_HEADS * LANES),
                  pl.BlockSpec((1, N_KV_HEADS * V_AUG, seq), lambda b, j: (b, 0, 0)),
                  _full(corr.shape), _full(brow.shape)],
        out_specs=qblk(ATTN_WIDTH),
        out_shape=jax.ShapeDtypeStruct((nb, nq, ATTN_WIDTH, QBLK), BF16),
        scratch_shapes=[pltpu.VMEM((seq, QBLK), jnp.int32), pltpu.VMEM((seq, QBLK), F32),
                        pltpu.VMEM((N_KV_HEADS, 1, GROUP * QBLK), F32),
                        pltpu.VMEM((N_KV_HEADS, V_AUG, GROUP * QBLK), F32),
                        pltpu.VMEM((N_KV_HEADS, CHUNK, GROUP * QBLK), F32),
                        pltpu.VMEM((N_KV_HEADS, CHUNK, GROUP * QBLK), F32),
                        pltpu.VMEM((N_KV_HEADS, 1, GROUP * QBLK), F32),
                        pltpu.VMEM((N_KV_HEADS, 1, GROUP * QBLK), F32),
                        pltpu.VMEM((N_KV_HEADS, CHUNK, GROUP * QBLK), BF16),
                        pltpu.VMEM((4 * IDX_DIM, IDX_HEADS * QBLK), BF16),
                        pltpu.VMEM((N_KV_HEADS, LANES, GROUP * QBLK), BF16)],
        compiler_params=pltpu.CompilerParams(dimension_semantics=("arbitrary", "arbitrary"),
                                             vmem_limit_bytes=VMEM_LIMIT),
        name="sparse_attn",
    )(qp, wiT, kis, ka, vTa, corr, brow)


def _merge_ffn(x2, mp, ga, aaT, wap, wout, ln2, wgu, wd, lnf, seq, tm):
    t = x2.shape[0]
    tps = seq // tm
    row = lambda n: pl.BlockSpec((tm, n), lambda i: (i, 0))
    return pl.pallas_call(
        functools.partial(_ffn_kernel, fc=256),
        grid=(t // tm,),
        in_specs=[row(D_MODEL), row(D_MODEL), row(D_MODEL),
                  pl.BlockSpec((1, tm // QBLK, ATTN_WIDTH, QBLK), lambda i: (i // tps, i % tps, 0, 0)),
                  _full(wap.shape), _full(wout.shape), _full((1, D_MODEL)), _full(wgu.shape), _full(wd.shape),
                  _full((1, D_MODEL))],
        out_specs=row(D_MODEL),
        out_shape=jax.ShapeDtypeStruct((t, D_MODEL), F32),
        compiler_params=pltpu.CompilerParams(dimension_semantics=("arbitrary",), vmem_limit_bytes=VMEM_LIMIT),
        name="merge_ffn",
    )(x2, mp, ga, aaT, wap, wout, ln2, wgu, wd, lnf)


def _sample_scores(page_table, qi, wi4, cache_ikT, pages):
    bd, n_pages = page_table.shape
    page_spec = lambda i: pl.BlockSpec((1, 1, IDX_DIM, PAGE_SIZE), lambda b, c, pt: (0, pt[b, c * pages + i], 0, 0))
    grid_spec = pltpu.PrefetchScalarGridSpec(
        num_scalar_prefetch=1,
        grid=(bd, n_pages // pages),
        in_specs=[pl.BlockSpec((1, IDX_HEADS, IDX_DIM), lambda b, c, pt: (b, 0, 0)),
                  pl.BlockSpec((1, IDX_HEADS, 1), lambda b, c, pt: (b, 0, 0))] + [page_spec(i) for i in range(pages)],
        out_specs=pl.BlockSpec((1, 1, pages * PAGE_SIZE), lambda b, c, pt: (b, 0, c)),
    )
    return pl.pallas_call(
        functools.partial(_sscore_kernel, pages=pages),
        grid_spec=grid_spec,
        out_shape=jax.ShapeDtypeStruct((bd, 1, n_pages * PAGE_SIZE), F32),
        compiler_params=pltpu.CompilerParams(dimension_semantics=("arbitrary", "arbitrary")),
        name="sample_scores",
    )(page_table, qi, wi4, *([cache_ikT] * pages))


def _sample_select(scores, qi, ki_new, wi, n_past, cw, k_top):
    bd = scores.shape[0]
    width = n_past + LANES
    return pl.pallas_call(
        functools.partial(_sselect_kernel, n_past=n_past, cw=cw, k_top=k_top),
        out_shape=jax.ShapeDtypeStruct((bd, width), F32),
        scratch_shapes=[pltpu.VMEM((bd, width), jnp.int32)],
        compiler_params=pltpu.CompilerParams(vmem_limit_bytes=VMEM_LIMIT),
        name="sample_select",
    )(scores, qi, ki_new, wi)


def _sample_attn(page_table, qbd, madd, madd_new, sbias, sbias_new, k_new, v_new, cache_kT, cache_vT, pages):
    bd, n_pages = page_table.shape
    chunk = pages * PAGE_SIZE
    page_spec = lambda i: pl.BlockSpec((1, 1, N_KV_HEADS, HEAD_DIM, PAGE_SIZE),
                                       lambda b, c, pt: (0, pt[b, c * pages + i], 0, 0, 0))
    per_b = lambda shape: pl.BlockSpec((1,) + shape, lambda b, c, pt: (b, 0, 0))
    grid_spec = pltpu.PrefetchScalarGridSpec(
        num_scalar_prefetch=1,
        grid=(bd, n_pages // pages),
        in_specs=[per_b((N_HEADS, KV_WIDTH)),
                  pl.BlockSpec((1, 1, chunk), lambda b, c, pt: (b, 0, c)),
                  per_b((1, LANES)),
                  pl.BlockSpec((N_HEADS, chunk), lambda b, c, pt: (0, c)),
                  pl.BlockSpec((N_HEADS, LANES), lambda b, c, pt: (0, 0)),
                  per_b((1, KV_WIDTH)), per_b((1, KV_WIDTH))]
                 + [page_spec(i) for i in range(pages)] * 2,
        out_specs=per_b((N_HEADS, KV_WIDTH)),
        scratch_shapes=[pltpu.VMEM((N_HEADS, 1), F32), pltpu.VMEM((N_HEADS, 1), F32),
                        pltpu.VMEM((N_HEADS, KV_WIDTH), F32)],
    )
    return pl.pallas_call(
        functools.partial(_sattn_kernel, pages=pages),
        grid_spec=grid_spec,
        out_shape=jax.ShapeDtypeStruct((bd, N_HEADS, KV_WIDTH), F32),
        compiler_params=pltpu.CompilerParams(dimension_semantics=("arbitrary", "arbitrary"),
                                             vmem_limit_bytes=VMEM_LIMIT),
        name="sample_attn",
    )(page_table, qbd, madd, madd_new, sbias, sbias_new, k_new, v_new, *([cache_kT] * pages), *([cache_vT] * pages))


def _pick(n, prefs):
    for p in prefs:
        if n % p == 0:
            return p
    return n


def kernel(x_prompt, x_sample, cache_k, cache_v, cache_idx_k, state_pool, page_table, ln1, w_in, pool_w, pool_scale,
           w_pool_proj, w_attn_proj, w_out, ln2, w_gate_up, w_down, rel_bias, ln_final):
    assert w_in.shape[0] == 1, "one layer"
    nb, seq, _ = x_prompt.shape
    bd, dec_seq, _ = x_sample.shape
    assert dec_seq == 1 and seq % CHUNK == 0
    n_pages = page_table.shape[1]
    n_past = n_pages * PAGE_SIZE

    w_nat, w_t = _pack_w_in(w_in[0])
    ln1r, ln2r, lnfr = ln1[0][None], ln2[0][None], ln_final[None]
    pw = pool_w[0].astype(BF16)
    ps = pool_scale[0][None]
    wpp, wap, wout = w_pool_proj[0].astype(BF16), w_attn_proj[0].astype(BF16), w_out[0].astype(BF16)
    wgu, wd = w_gate_up[0].astype(BF16), w_down[0].astype(BF16)
    corr, brow, srow = _bias_tables(rel_bias)

    xp = x_prompt.reshape(nb * seq, D_MODEL)
    qp, kT, vT, vTa, kiT, wiT, kis, ka, mp, ga, ul = _proj_prompt(
        xp, ln1r, w_nat, w_t, pw, ps, wpp, nb, seq, _pick(seq, (512, 256, 128)))
    aaT = _attn_prompt(qp, wiT, kis, ka, vTa, corr, brow, nb, seq, min(TOPK_MAX, seq // 4))
    y_prompt = _merge_ffn(xp, mp, ga, aaT, wap, wout, ln2r, wgu, wd, lnfr, seq, _pick(seq, (256, 128)))
    y_prompt = y_prompt.reshape(nb, seq, D_MODEL)

    xs = x_sample.reshape(bd, D_MODEL)
    st = jnp.swapaxes(state_pool[0], 0, 1)
    qs, ks, vs, kin, qin, wis, mps, gas, us = _proj_sample(xs, ln1r, w_nat, w_t, pw, ps, wpp, st, n_past)
    pages = _pick(n_pages, (32, 16, 8, 4, 2))
    scores = _sample_scores(page_table, qin.reshape(bd, IDX_HEADS, IDX_DIM), wis[:, :IDX_HEADS, None],
                            jnp.swapaxes(cache_idx_k, 2, 3), _pick(n_pages, (32, 16, 8, 4, 2))).reshape(bd, n_past)
    width = n_past + LANES
    madd = _sample_select(scores, qin, kin, wis, n_past, _pick(width, (640, 512, 384, 256, 128)),
                          min(TOPK_MAX, (n_past + 1) // 4))
    head_kv = np.arange(N_HEADS) // GROUP
    onehot = jnp.asarray(head_kv[:, None] == np.arange(N_KV_HEADS)[None, :])
    qbd = jnp.where(onehot[None, :, :, None], qs.reshape(bd, N_HEADS, 1, HEAD_DIM), 0).reshape(bd, N_HEADS, KV_WIDTH)
    far = jnp.broadcast_to(srow[:, 2 * LANES:2 * LANES + 1], (N_HEADS, n_past - LANES))
    sbias = jnp.concatenate([far, srow[:, :LANES]], axis=1)
    ao = _sample_attn(page_table, qbd, madd[:, None, :n_past], madd[:, None, n_past:], sbias, srow[:, LANES:2 * LANES],
                      ks[:, None, :], vs[:, None, :], jnp.transpose(cache_k, (0, 1, 3, 4, 2)),
                      jnp.transpose(cache_v, (0, 1, 3, 4, 2)), pages)
    aas = jnp.sum(jnp.where(onehot[None, :, :, None], ao.reshape(bd, N_HEADS, N_KV_HEADS, HEAD_DIM), 0), axis=2)
    aasT = aas.reshape(bd // QBLK, QBLK, ATTN_WIDTH).astype(BF16).transpose(0, 2, 1)[None]
    y_sample = _merge_ffn(xs, mps, gas, aasT, wap, wout, ln2r, wgu, wd, lnfr, bd, bd).reshape(bd, 1, D_MODEL)

    to_heads = lambda a: jnp.transpose(a.reshape(nb, N_KV_HEADS, HEAD_DIM, seq), (0, 3, 1, 2))[None]
    return (y_prompt, y_sample, to_heads(kT), to_heads(vT), jnp.swapaxes(kiT, 1, 2)[None], ul[None, :, 1:, :],
            ks.reshape(1, bd, 1, N_KV_HEADS, HEAD_DIM), vs.reshape(1, bd, 1, N_KV_HEADS, HEAD_DIM),
            kin.reshape(1, bd, 1, IDX_DIM),
            jnp.concatenate([state_pool[0][:, 1:], us[:, None, :]], axis=1)[None])
```

```python
import functools
import math

import jax
import jax.numpy as jnp
import numpy as np
from jax import lax
from jax.experimental import pallas as pl
from jax.experimental.pallas import tpu as pltpu

D_MODEL = 1024
PAGE_SIZE = 128
POOL_WIDTH = D_MODEL // 2
POOL_WINDOWS = (2, 4, 8, 16)
POOL_GROUP_DIM = POOL_WIDTH // len(POOL_WINDOWS)
POOL_BUF = 15
N_HEADS = 8
N_KV_HEADS = 4
HEAD_DIM = 64
GROUP = N_HEADS // N_KV_HEADS
ATTN_WIDTH = N_HEADS * HEAD_DIM
KV_WIDTH = N_KV_HEADS * HEAD_DIM
IDX_HEADS = 4
IDX_DIM = 64
TOPK_MAX = 256
N_BUCKETS = 32
MAX_DISTANCE = 128
D_FF = -(-(8 * D_MODEL) // (3 * 256)) * 256
EPS = 1e-6
IN_SPLITS = (POOL_WIDTH, ATTN_WIDTH, KV_WIDTH, KV_WIDTH, IDX_HEADS * IDX_DIM, IDX_DIM, IDX_HEADS, 2 * D_MODEL)

LANES = 128
SUBLANES = 8
QBLK = 256
CHUNK = 512
INT_MIN = -(2 ** 31)
NEG = -1e30
LOG2E = math.log2(math.e)
VMEM_LIMIT = 56 * 1024 * 1024
V_AUG = HEAD_DIM + 16

N_U, N_KA, N_KK, N_G = 0, 512, 1024, 1152
N_END = N_G + 2 * D_MODEL
T_Q, T_K, T_V, T_QI, T_KI, T_WI = 0, 512, 768, 1024, 1280, 1344
T_END = T_WI + SUBLANES
QP_ROWS = ATTN_WIDTH + IDX_HEADS * LANES

F32 = jnp.float32
BF16 = jnp.bfloat16


def _dot(a, b, precision=None):
    return jnp.dot(a, b, preferred_element_type=F32, precision=precision)


def _dot_nt(a, b):
    return lax.dot_general(a, b, (((1,), (1,)), ((), ())), preferred_element_type=F32)


def _dot_tn(a, b):
    return lax.dot_general(a, b, (((0,), (0,)), ((), ())), preferred_element_type=F32)


def _rms(x, g):
    return (x * lax.rsqrt(jnp.mean(x * x, axis=-1, keepdims=True) + EPS)) * g


def _hi_lo(z):
    hi = z.astype(BF16).astype(F32)
    return hi, z - hi


def _sortable_key(score):
    bits = lax.bitcast_convert_type(score, jnp.int32)
    return jnp.where(bits < 0, -(bits & 0x7FFFFFFF), bits)


def _gated_pool(d_groups, gates, pw_ref, ps_ref, wpp_ref):
    a = [_dot(d.astype(BF16), pw_ref[g]) for g, d in enumerate(d_groups)]
    a = jnp.concatenate(a, axis=1) * ps_ref[...]
    ap = _dot(a.astype(BF16), wpp_ref[...])
    return jax.nn.sigmoid(gates[:, :D_MODEL]) * ap


def _proj_kernel(x_ref, ln1_ref, wn_ref, wt_ref, pw_ref, ps_ref, wpp_ref,
                 qp_ref, kT_ref, vT_ref, vTa_ref, kiT_ref, wiT_ref, kis_ref, ka_ref, mp_ref, ga_ref, ul_ref,
                 uext_ref, *, tm, tiles_per_seq):
    ti = pl.program_id(0) % tiles_per_seq
    hb = _rms(x_ref[...], ln1_ref[...]).astype(BF16)

    def mm(lo, hi):
        return _dot(hb, wn_ref[:, lo:hi])

    def mt(lo, hi):
        return _dot_nt(wt_ref[lo:hi, :], hb)

    def per_qblk(ref, row0, val):
        for i in range(tm // QBLK):
            ref[0, i, row0:row0 + val.shape[0], :] = val[:, i * QBLK:(i + 1) * QBLK]

    per_qblk(qp_ref, 0, (mt(T_Q, T_K) * (HEAD_DIM ** -0.5 * LOG2E)).astype(BF16))
    kT_ref[0] = mt(T_K, T_V)
    vT = mt(T_V, T_QI)
    vT_ref[0] = vT
    ones = jnp.ones((V_AUG - HEAD_DIM, tm), F32)
    vTa_ref[0] = jnp.concatenate(
        [piece for c in range(N_KV_HEADS) for piece in (vT[c * HEAD_DIM:(c + 1) * HEAD_DIM], ones)], axis=0).astype(BF16)
    hi, lo = _hi_lo(mt(T_QI, T_KI))
    per_qblk(qp_ref, ATTN_WIDTH, jnp.concatenate(
        [piece for h in range(IDX_HEADS) for piece in (hi[h * IDX_DIM:(h + 1) * IDX_DIM], lo[h * IDX_DIM:(h + 1) * IDX_DIM])],
        axis=0).astype(BF16))
    kiT_ref[0] = mt(T_KI, T_WI)
    per_qblk(wiT_ref, 0, mt(T_WI, T_END) * IDX_HEADS ** -0.5)

    hi, lo = _hi_lo(mm(N_KK, N_G))
    kis_ref[...] = jnp.concatenate([hi, lo], axis=1).astype(BF16)
    lane = lax.broadcasted_iota(jnp.int32, (1, N_KV_HEADS * LANES), 1) % LANES
    bias_cols = jnp.where((lane == HEAD_DIM) | (lane == HEAD_DIM + 1), 1.0, 0.0)
    ka_ref[...] = (mm(N_KA, N_KK) + bias_cols).astype(BF16)

    u = mm(N_U, N_KA)

    @pl.when(ti == 0)
    def _():
        uext_ref[0:16, :] = jnp.zeros((16, POOL_WIDTH), F32)

    @pl.when(ti != 0)
    def _():
        uext_ref[0:16, :] = uext_ref[tm:tm + 16, :]

    uext_ref[16:16 + tm, :] = u
    ul_ref[0] = uext_ref[tm:tm + 16, :]
    pos1 = ti * tm + lax.broadcasted_iota(jnp.int32, (tm, 1), 0) + 1
    ds = []
    for g, w in enumerate(POOL_WINDOWS):
        sl = slice(g * POOL_GROUP_DIM, (g + 1) * POOL_GROUP_DIM)
        win = uext_ref[16:16 + tm, sl]
        for d in range(1, w):
            win = win + uext_ref[16 - d:16 - d + tm, sl]
        cnt = jnp.minimum(w, pos1).astype(F32)
        ds.append(win / cnt - u[:, sl])
    gates = mm(N_G, N_END)
    mp_ref[...] = _gated_pool(ds, gates, pw_ref, ps_ref, wpp_ref)
    ga_ref[...] = jax.nn.sigmoid(gates[:, D_MODEL:])


def _sproj_kernel(x_ref, ln1_ref, wn_ref, wt_ref, pw_ref, ps_ref, wpp_ref, st_ref,
                  q_ref, k_ref, v_ref, ki_ref, qi_ref, wi_ref, mp_ref, ga_ref, u_ref, *, n_past):
    hb = _rms(x_ref[...], ln1_ref[...]).astype(BF16)

    def mm(lo, hi):
        return _dot(hb, wn_ref[:, lo:hi])

    def mn(lo, hi):
        return _dot_nt(hb, wt_ref[lo:hi, :])

    q_ref[...] = (mn(T_Q, T_K) * HEAD_DIM ** -0.5).astype(BF16)
    k_ref[...] = mn(T_K, T_V)
    v_ref[...] = mn(T_V, T_QI)
    qi_ref[...] = mn(T_QI, T_KI)
    ki_ref[...] = mn(T_KI, T_WI)
    wi_ref[...] = mn(T_WI, T_END) * IDX_HEADS ** -0.5
    u = mm(N_U, N_KA)
    u_ref[...] = u
    ds = []
    for g, w in enumerate(POOL_WINDOWS):
        sl = slice(g * POOL_GROUP_DIM, (g + 1) * POOL_GROUP_DIM)
        win = u[:, sl]
        for d in range(1, w):
            win = win + st_ref[POOL_BUF - d][:, sl]
        ds.append(win / float(min(w, n_past + 1)) - u[:, sl])
    gates = mm(N_G, N_END)
    mp_ref[...] = _gated_pool(ds, gates, pw_ref, ps_ref, wpp_ref)
    ga_ref[...] = jax.nn.sigmoid(gates[:, D_MODEL:])


def _bias_kernel(rb_ref, corr_ref, brow_ref, srow_ref):
    i = lax.broadcasted_iota(jnp.int32, (LANES, LANES), 0)
    j = lax.broadcasted_iota(jnp.int32, (LANES, LANES), 1)
    max_exact = N_BUCKETS // 2

    def bias_of(dist, h):
        d = jnp.maximum(dist, 1).astype(F32)
        large = max_exact + (jnp.log(d / max_exact) / math.log(MAX_DISTANCE / max_exact)
                             * (N_BUCKETS - max_exact)).astype(jnp.int32)
        bucket = jnp.where(dist < max_exact, dist, jnp.minimum(large, N_BUCKETS - 1))
        acc = jnp.zeros(dist.shape, F32)
        for b in range(N_BUCKETS):
            acc = jnp.where(bucket == b, rb_ref[b, h], acc)
        return acc

    row = lax.broadcasted_iota(jnp.int32, (HEAD_DIM, QBLK), 0)
    lane1 = lax.broadcasted_iota(jnp.int32, (1, LANES), 1)
    for h in range(N_HEADS):
        far = rb_ref[N_BUCKETS - 1, h]
        for t in range(2):
            corr_ref[t, h] = (bias_of(jnp.maximum(t * LANES + j - i, 0), h) - far) * LOG2E
        corr_ref[2, h] = jnp.zeros((LANES, LANES), F32)
        far_v = jnp.full((HEAD_DIM, QBLK), far, F32) * LOG2E
        hi, lo = _hi_lo(far_v)
        brow_ref[h] = jnp.where(row == 0, hi, jnp.where(row == 1, lo, 0.0))
        srow_ref[h:h + 1, :] = jnp.concatenate([bias_of(LANES - lane1, h), bias_of(jnp.zeros((1, LANES), jnp.int32), h),
                                       jnp.full((1, LANES), far, F32)], axis=1)


def _topk_mask_cols(keys_ref, madd_ref, nc, cw, k_top):
    part = 4 * SUBLANES
    width = keys_ref.shape[1]

    def count_ge(thr):
        def body(c, acc):
            kk = keys_ref[pl.ds(pl.multiple_of(c * cw, cw), cw), :]
            for i in range(cw // part):
                acc = jnp.where(kk[i * part:(i + 1) * part] >= thr, acc + 1, acc)
            return acc

        acc = lax.fori_loop(0, nc, body, jnp.zeros((part, width), jnp.int32))
        return jnp.sum(acc, axis=0, keepdims=True)

    def bit_step(it, thr):
        cand = thr + lax.shift_left(jnp.int32(1), 31 - it)
        return jnp.where(count_ge(cand) >= k_top, cand, thr)

    thr = lax.fori_loop(0, 32, bit_step, jnp.full((1, width), INT_MIN, jnp.int32))
    thr_eff = jnp.maximum(thr, INT_MIN + 1)

    def mask_body(c, acc):
        off = pl.multiple_of(c * cw, cw)
        sel = keys_ref[pl.ds(off, cw), :] >= thr_eff
        madd_ref[pl.ds(off, cw), :] = jnp.where(sel, 0.0, NEG)
        return acc + jnp.sum(jnp.where(sel, 1, 0).reshape(cw // part, part, width), axis=0)

    acc = lax.fori_loop(0, nc, mask_body, jnp.zeros((part, width), jnp.int32))
    n_sel = jnp.sum(acc, axis=0, keepdims=True)

    @pl.when(jnp.max(n_sel) > k_top)
    def _():
        room = (k_top - count_ge(thr + 1)).astype(F32)
        a = lax.broadcasted_iota(jnp.int32, (LANES, LANES), 0)
        b = lax.broadcasted_iota(jnp.int32, (LANES, LANES), 1)
        lower = jnp.where(b <= a, 1.0, 0.0).astype(BF16)

        def tie_body(c, seen):
            off = pl.multiple_of(c * LANES, LANES)
            kk = keys_ref[pl.ds(off, LANES), :]
            eq = kk == thr
            ones = jnp.where(eq, 1.0, 0.0)
            rank = _dot(lower, ones.astype(BF16)) + seen
            tied = jnp.where(rank <= room, 0.0, NEG)
            tied = jnp.where(kk > INT_MIN, tied, NEG)
            madd_ref[pl.ds(off, LANES), :] = jnp.where(eq, tied, jnp.where(kk > thr, 0.0, NEG))
            return seen + jnp.sum(ones, axis=0, keepdims=True)

        lax.fori_loop(0, nc * (cw // LANES), tie_body, jnp.zeros((1, width), F32))


def _topk_mask_rows(keys_ref, madd_ref, nc, cw, k_top):
    rows = keys_ref.shape[0]
    sub = cw // LANES

    def count_ge(thr):
        thr_b = jnp.broadcast_to(thr, (rows, LANES))

        def body(c, acc):
            off = pl.multiple_of(c * cw, LANES)
            for s in range(sub):
                kk = keys_ref[:, pl.ds(off + s * LANES, LANES)]
                acc = acc + jnp.where(kk >= thr_b, 1, 0)
            return acc

        acc = lax.fori_loop(0, nc, body, jnp.zeros((rows, LANES), jnp.int32))
        return jnp.sum(acc, axis=1, keepdims=True)

    def bit_step(it, thr):
        cand = thr + lax.shift_left(jnp.int32(1), 31 - it)
        return jnp.where(count_ge(cand) >= k_top, cand, thr)

    thr = lax.fori_loop(0, 32, bit_step, jnp.full((rows, 1), INT_MIN, jnp.int32))
    thr_b = jnp.broadcast_to(jnp.maximum(thr, INT_MIN + 1), (rows, LANES))

    def mask_body(c, acc):
        off = pl.multiple_of(c * cw, LANES)
        for s in range(sub):
            kk = keys_ref[:, pl.ds(off + s * LANES, LANES)]
            sel = kk >= thr_b
            madd_ref[:, pl.ds(off + s * LANES, LANES)] = jnp.where(sel, 0.0, NEG)
            acc = acc + jnp.where(sel, 1, 0)
        return acc

    acc = lax.fori_loop(0, nc, mask_body, jnp.zeros((rows, LANES), jnp.int32))
    n_sel = jnp.sum(acc, axis=1, keepdims=True)

    @pl.when(jnp.max(n_sel) > k_top)
    def _():
        room = (k_top - count_ge(thr + 1)).astype(F32)
        a = lax.broadcasted_iota(jnp.int32, (LANES, LANES), 0)
        b = lax.broadcasted_iota(jnp.int32, (LANES, LANES), 1)
        upper = jnp.where(a <= b, 1.0, 0.0).astype(BF16)
        eq_thr = jnp.broadcast_to(thr, (rows, LANES))

        def tie_body(c, seen):
            off = pl.multiple_of(c * LANES, LANES)
            kk = keys_ref[:, pl.ds(off, LANES)]
            eq = kk == eq_thr
            ones = jnp.where(eq, 1.0, 0.0)
            rank = _dot(ones.astype(BF16), upper) + seen
            tied = jnp.where(rank <= room, 0.0, NEG)
            tied = jnp.where(kk > INT_MIN, tied, NEG)
            madd_ref[:, pl.ds(off, LANES)] = jnp.where(eq, tied, jnp.where(kk > eq_thr, 0.0, NEG))
            return seen + jnp.sum(ones, axis=1, keepdims=True)

        lax.fori_loop(0, nc * sub, tie_body, jnp.zeros((rows, 1), F32))


def _attn_kernel(qp_ref, wiT_ref, kis_ref, ka_ref, vTa_ref, corr_ref, brow_ref, o_ref,
                 keys_ref, madd_ref, m_ref, acc_ref, s0_ref, s1_ref, cm0_ref, cm1_ref, p_ref, qs_ref, qa_ref, *, k_top):
    j = pl.program_id(1)
    t0 = j * QBLK
    cj = t0 // CHUNK
    nc = cj + 1
    qpos = t0 + lax.broadcasted_iota(jnp.int32, (1, QBLK), 1)

    for h in range(IDX_HEADS):
        rows = slice(ATTN_WIDTH + h * LANES, ATTN_WIDTH + (h + 1) * LANES)
        qs_ref[h] = jnp.concatenate([qp_ref[0, 0, rows, :]] * 2, axis=0)

    def score_body(c, carry):
        off = pl.multiple_of(c * CHUNK, CHUNK)
        sc = jnp.zeros((CHUNK, QBLK), F32)
        for h in range(IDX_HEADS):
            s = _dot(kis_ref[pl.ds(off, CHUNK), :], qs_ref[h]) * IDX_DIM ** -0.5
            sc = sc + wiT_ref[0, 0, h:h + 1, :] * jnp.maximum(s, 0.0)
        kpos = off + lax.broadcasted_iota(jnp.int32, (CHUNK, 1), 0)
        keys_ref[pl.ds(off, CHUNK), :] = jnp.where(kpos <= qpos, _sortable_key(sc), INT_MIN)
        return carry

    lax.fori_loop(0, nc, score_body, 0)
    _topk_mask_cols(keys_ref, madd_ref, nc, CHUNK, k_top)

    for h in range(N_HEADS):
        qa_ref[h] = jnp.concatenate([qp_ref[0, 0, h * HEAD_DIM:(h + 1) * HEAD_DIM, :], brow_ref[h].astype(BF16)], axis=0)
    m_ref[...] = jnp.full(m_ref.shape, NEG, F32)
    acc_ref[...] = jnp.zeros(acc_ref.shape, F32)

    nblk = CHUNK // LANES
    nsub = QBLK // LANES
    s_bufs, cm_bufs = (s0_ref, s1_ref), (cm0_ref, cm1_ref)

    def logits(c, near, s_ref, cm_ref):
        off = pl.multiple_of(c * CHUNK, CHUNK)
        for h in range(N_HEADS):
            g = h // GROUP
            mx = jnp.full((SUBLANES, QBLK), NEG, F32)
            for i in range(nblk):
                rows = pl.ds(off + i * LANES, LANES)
                s = _dot(ka_ref[rows, g * LANES:(g + 1) * LANES], qa_ref[h]) + madd_ref[rows, :]
                if near:
                    tiles = [corr_ref[jnp.clip(j * nsub + q - (c * nblk + i), 0, 2), h] for q in range(nsub)]
                    s = s + jnp.concatenate(tiles, axis=1)
                s_ref[h, i * LANES:(i + 1) * LANES, :] = s
                mx = jnp.maximum(mx, jnp.max(s.reshape(LANES // SUBLANES, SUBLANES, QBLK), axis=0))
            cm_ref[h] = jnp.max(mx, axis=0, keepdims=True)

    def accumulate(c, s_ref, cm_ref):
        off = pl.multiple_of(c * CHUNK, CHUNK)
        for h in range(N_HEADS):
            g = h // GROUP
            m_prev = m_ref[h]
            m_next = jnp.maximum(m_prev, cm_ref[h])
            for i in range(nblk):
                blk = slice(i * LANES, (i + 1) * LANES)
                p_ref[h, blk, :] = jnp.exp2(s_ref[h, blk, :] - m_next).astype(BF16)
            pv = _dot(vTa_ref[0, g * V_AUG:(g + 1) * V_AUG, pl.ds(off, CHUNK)], p_ref[h])
            acc_ref[h] = jnp.exp2(m_prev - m_next) * acc_ref[h] + pv
            m_ref[h] = m_next

    def step(c, near):
        for par in (0, 1):
            @pl.when(c % 2 == par)
            def _(par=par):
                logits(c + 1, near, s_bufs[1 - par], cm_bufs[1 - par])
                accumulate(c, s_bufs[par], cm_bufs[par])

    def far_body(c, carry):
        step(c, False)
        return carry

    def near_body(c, carry):
        step(c, True)
        return carry

    n_far = jnp.maximum(cj - 2, 0)
    logits(0, True, s_bufs[0], cm_bufs[0])
    lax.fori_loop(0, n_far, far_body, 0)
    lax.fori_loop(n_far, cj, near_body, 0)
    for par in (0, 1):
        @pl.when(cj % 2 == par)
        def _(par=par):
            accumulate(cj, s_bufs[par], cm_bufs[par])

    for h in range(N_HEADS):
        acc = acc_ref[h]
        o_ref[0, 0, h * HEAD_DIM:(h + 1) * HEAD_DIM, :] = (acc[0:HEAD_DIM] / acc[HEAD_DIM:HEAD_DIM + 1]).astype(o_ref.dtype)


def _ffn_kernel(x_ref, mp_ref, ga_ref, aaT_ref, wap_ref, wout_ref, ln2_ref, wgu_ref, wd_ref, lnf_ref, y_ref, *, fc):
    aaT = jnp.concatenate([aaT_ref[0, i] for i in range(aaT_ref.shape[1])], axis=1)
    att = _dot_tn(aaT, wap_ref[...])
    m = mp_ref[...] + ga_ref[...] * att
    h = x_ref[...] + _dot(m.astype(BF16), wout_ref[...])
    hn = _rms(h, ln2_ref[...]).astype(BF16)
    y = h
    for c in range(D_FF // fc):
        gate = _dot(hn, wgu_ref[:, c * fc:(c + 1) * fc])
        up = _dot(hn, wgu_ref[:, D_FF + c * fc:D_FF + (c + 1) * fc])
        act = (gate * jax.nn.sigmoid(gate)) * up
        y = y + _dot(act.astype(BF16), wd_ref[c * fc:(c + 1) * fc, :])
    y_ref[...] = _rms(y, lnf_ref[...])


def _sscore_kernel(pt_ref, qi_ref, wi_ref, *refs, pages):
    page_refs, o_ref = refs[:pages], refs[pages]
    kc = jnp.concatenate([r[0, 0] for r in page_refs], axis=1)
    s = _dot(qi_ref[0], kc, precision=lax.Precision.HIGHEST) * IDX_DIM ** -0.5
    o_ref[0] = jnp.sum(wi_ref[0] * jnp.maximum(s, 0.0), axis=0, keepdims=True)


def _sselect_kernel(sc_ref, qi_ref, kin_ref, wi_ref, madd_ref, keys_ref, *, n_past, cw, k_top):
    rows = sc_ref.shape[0]
    keys_ref[:, 0:n_past] = _sortable_key(sc_ref[...])
    kin = kin_ref[...]
    wi = wi_ref[...]
    s_new = jnp.zeros((rows, 1), F32)
    for h in range(IDX_HEADS):
        s = jnp.sum(qi_ref[:, h * IDX_DIM:(h + 1) * IDX_DIM] * kin, axis=1, keepdims=True) * IDX_DIM ** -0.5
        s_new = s_new + wi[:, h:h + 1] * jnp.maximum(s, 0.0)
    lane = lax.broadcasted_iota(jnp.int32, (rows, keys_ref.shape[1] - n_past), 1)
    keys_ref[:, n_past:] = jnp.where(lane == 0, _sortable_key(s_new), INT_MIN)
    _topk_mask_rows(keys_ref, madd_ref, keys_ref.shape[1] // cw, cw, k_top)


def _sattn_kernel(pt_ref, qbd_ref, madd_ref, maddn_ref, sb_ref, sbn_ref, kn_ref, vn_ref, *refs, pages):
    k_refs, v_refs = refs[:pages], refs[pages:2 * pages]
    o_ref, m_ref, l_ref, acc_ref = refs[2 * pages:]
    c = pl.program_id(1)

    @pl.when(c == 0)
    def _():
        m_ref[...] = jnp.full(m_ref.shape, NEG, F32)
        l_ref[...] = jnp.zeros(l_ref.shape, F32)
        acc_ref[...] = jnp.zeros(acc_ref.shape, F32)

    qbd = qbd_ref[0]
    kc = jnp.concatenate([r[0, 0].reshape(KV_WIDTH, PAGE_SIZE) for r in k_refs], axis=1).astype(BF16)
    vc = jnp.concatenate([r[0, 0].reshape(KV_WIDTH, PAGE_SIZE) for r in v_refs], axis=1).astype(BF16)
    s = _dot(qbd, kc) + sb_ref[...] + madd_ref[0]
    m_prev = m_ref[...]
    m_next = jnp.maximum(m_prev, jnp.max(s, axis=1, keepdims=True))
    alpha = jnp.exp(m_prev - m_next)
    p = jnp.exp(s - m_next)
    l_ref[...] = alpha * l_ref[...] + jnp.sum(p, axis=1, keepdims=True)
    acc_ref[...] = alpha * acc_ref[...] + _dot_nt(p.astype(BF16), vc)
    m_ref[...] = m_next

    @pl.when(c == pl.num_programs(1) - 1)
    def _():
        s_new = (jnp.sum(qbd.astype(F32) * kn_ref[0], axis=1, keepdims=True)
                 + sbn_ref[:, 0:1] + maddn_ref[0][:, 0:1])
        m_prev = m_ref[...]
        m_next = jnp.maximum(m_prev, s_new)
        alpha = jnp.exp(m_prev - m_next)
        p = jnp.exp(s_new - m_next)
        l = alpha * l_ref[...] + p
        o_ref[0] = (alpha * acc_ref[...] + p * vn_ref[0]) / l


def _full(shape):
    return pl.BlockSpec(shape, lambda *_: (0,) * len(shape))


def _pack_w_in(w):
    offs = np.cumsum((0,) + IN_SPLITS)
    wu, wq, wk, wv, wqi, wki, wwi, wg = [w[:, offs[i]:offs[i + 1]] for i in range(len(IN_SPLITS))]
    pad = jnp.zeros((D_MODEL, LANES - HEAD_DIM), w.dtype)
    wka = jnp.concatenate([piece for c in range(N_KV_HEADS) for piece in (wk[:, c * HEAD_DIM:(c + 1) * HEAD_DIM], pad)], axis=1)
    w_nat = jnp.concatenate([wu, wka, wki, wki, wg], axis=1).astype(BF16)
    wwp = jnp.pad(wwi, ((0, 0), (0, SUBLANES - IDX_HEADS)))
    w_t = jnp.concatenate([wq, wk, wv, wqi, wki, wwp], axis=1).T.astype(BF16)
    return w_nat, w_t


def _proj_prompt(x2, ln1, w_nat, w_t, pw, ps, wpp, nb, seq, tm):
    t = x2.shape[0]
    tps = seq // tm
    row = lambda n: pl.BlockSpec((tm, n), lambda i: (i, 0))
    col = lambda n: pl.BlockSpec((1, n, tm), lambda i: (i // tps, 0, i % tps))
    tshape = lambda n, dt: jax.ShapeDtypeStruct((nb, n, seq), dt)
    qblk = lambda n: pl.BlockSpec((1, tm // QBLK, n, QBLK), lambda i: (i // tps, i % tps, 0, 0))
    out_shape = (
        jax.ShapeDtypeStruct((nb, seq // QBLK, QP_ROWS, QBLK), BF16),
        tshape(KV_WIDTH, F32), tshape(KV_WIDTH, F32),
        tshape(N_KV_HEADS * V_AUG, BF16),
        tshape(IDX_DIM, F32),
        jax.ShapeDtypeStruct((nb, seq // QBLK, SUBLANES, QBLK), F32),
        jax.ShapeDtypeStruct((t, 2 * LANES), BF16),
        jax.ShapeDtypeStruct((t, N_KV_HEADS * LANES), BF16),
        jax.ShapeDtypeStruct((t, D_MODEL), F32),
        jax.ShapeDtypeStruct((t, D_MODEL), F32),
        jax.ShapeDtypeStruct((nb, 16, POOL_WIDTH), F32),
    )
    out_specs = (qblk(QP_ROWS), col(KV_WIDTH), col(KV_WIDTH), col(N_KV_HEADS * V_AUG), col(IDX_DIM),
                 qblk(SUBLANES), row(2 * LANES), row(N_KV_HEADS * LANES), row(D_MODEL),
                 row(D_MODEL), pl.BlockSpec((1, 16, POOL_WIDTH), lambda i: (i // tps, 0, 0)))
    return pl.pallas_call(
        functools.partial(_proj_kernel, tm=tm, tiles_per_seq=tps),
        grid=(t // tm,),
        in_specs=[row(D_MODEL), _full((1, D_MODEL)), _full(w_nat.shape), _full(w_t.shape), _full(pw.shape),
                  _full((1, POOL_WIDTH)), _full(wpp.shape)],
        out_specs=out_specs,
        out_shape=out_shape,
        scratch_shapes=[pltpu.VMEM((tm + 16, POOL_WIDTH), F32)],
        compiler_params=pltpu.CompilerParams(dimension_semantics=("arbitrary",), vmem_limit_bytes=VMEM_LIMIT),
        name="proj_pool",
    )(x2, ln1, w_nat, w_t, pw, ps, wpp)


def _proj_sample(x2, ln1, w_nat, w_t, pw, ps, wpp, st, n_past):
    m = x2.shape[0]
    shapes = [(ATTN_WIDTH, BF16), (KV_WIDTH, F32), (KV_WIDTH, F32), (IDX_DIM, F32), (IDX_HEADS * IDX_DIM, F32),
              (SUBLANES, F32), (D_MODEL, F32), (D_MODEL, F32), (POOL_WIDTH, F32)]
    return pl.pallas_call(
        functools.partial(_sproj_kernel, n_past=n_past),
        out_shape=tuple(jax.ShapeDtypeStruct((m, n), dt) for n, dt in shapes),
        compiler_params=pltpu.CompilerParams(vmem_limit_bytes=VMEM_LIMIT),
        name="proj_pool_sample",
    )(x2, ln1, w_nat, w_t, pw, ps, wpp, st)


def _bias_tables(rel_bias):
    return pl.pallas_call(
        _bias_kernel,
        in_specs=[pl.BlockSpec(memory_space=pltpu.SMEM)],
        out_shape=(jax.ShapeDtypeStruct((3, N_HEADS, LANES, LANES), F32),
                   jax.ShapeDtypeStruct((N_HEADS, HEAD_DIM, QBLK), F32),
                   jax.ShapeDtypeStruct((N_HEADS, 3 * LANES), F32)),
        name="bias_tables",
    )(rel_bias)


def _attn_prompt(qp, wiT, kis, ka, vTa, corr, brow, nb, seq, k_top):
    nq = seq // QBLK
    qblk = lambda n: pl.BlockSpec((1, 1, n, QBLK), lambda b, j: (b, j, 0, 0))
    seqrow = lambda n: pl.BlockSpec((seq, n), lambda b, j: (b, 0))
    return pl.pallas_call(
        functools.partial(_attn_kernel, k_top=k_top),
        grid=(nb, nq),
        in_specs=[qblk(QP_ROWS), qblk(SUBLANES), seqrow(2 * LANES), seqrow(N_KV_HEADS * LANES),
                  pl.BlockSpec((1, N_KV_HEADS * V_AUG, seq), lambda b, j: (b, 0, 0)),
                  _full(corr.shape), _full(brow.shape)],
        out_specs=qblk(ATTN_WIDTH),
        out_shape=jax.ShapeDtypeStruct((nb, nq, ATTN_WIDTH, QBLK), BF16),
        scratch_shapes=[pltpu.VMEM((seq, QBLK), jnp.int32), pltpu.VMEM((seq, QBLK), F32),
                        pltpu.VMEM((N_HEADS, 1, QBLK), F32),
                        pltpu.VMEM((N_HEADS, V_AUG, QBLK), F32),
                        pltpu.VMEM((N_HEADS, CHUNK, QBLK), F32),
                        pltpu.VMEM((N_HEADS, CHUNK, QBLK), F32),
                        pltpu.VMEM((N_HEADS, 1, QBLK), F32),
                        pltpu.VMEM((N_HEADS, 1, QBLK), F32),
                        pltpu.VMEM((N_HEADS, CHUNK, QBLK), BF16),
                        pltpu.VMEM((IDX_HEADS, 4 * IDX_DIM, QBLK), BF16),
                        pltpu.VMEM((N_HEADS, LANES, QBLK), BF16)],
        compiler_params=pltpu.CompilerParams(dimension_semantics=("arbitrary", "arbitrary"),
                                             vmem_limit_bytes=VMEM_LIMIT),
        name="sparse_attn",
    )(qp, wiT, kis, ka, vTa, corr, brow)


def _merge_ffn(x2, mp, ga, aaT, wap, wout, ln2, wgu, wd, lnf, seq, tm):
    qb = aaT.shape[-1]
    t = x2.shape[0]
    tps = seq // tm
    row = lambda n: pl.BlockSpec((tm, n), lambda i: (i, 0))
    return pl.pallas_call(
        functools.partial(_ffn_kernel, fc=256),
        grid=(t // tm,),
        in_specs=[row(D_MODEL), row(D_MODEL), row(D_MODEL),
                  pl.BlockSpec((1, tm // qb, ATTN_WIDTH, qb), lambda i: (i // tps, i % tps, 0, 0)),
                  _full(wap.shape), _full(wout.shape), _full((1, D_MODEL)), _full(wgu.shape), _full(wd.shape),
                  _full((1, D_MODEL))],
        out_specs=row(D_MODEL),
        out_shape=jax.ShapeDtypeStruct((t, D_MODEL), F32),
        compiler_params=pltpu.CompilerParams(dimension_semantics=("arbitrary",), vmem_limit_bytes=VMEM_LIMIT),
        name="merge_ffn",
    )(x2, mp, ga, aaT, wap, wout, ln2, wgu, wd, lnf)


def _sample_scores(page_table, qi, wi4, cache_ikT, pages):
    bd, n_pages = page_table.shape
    page_spec = lambda i: pl.BlockSpec((1, 1, IDX_DIM, PAGE_SIZE), lambda b, c, pt: (0, pt[b, c * pages + i], 0, 0))
    grid_spec = pltpu.PrefetchScalarGridSpec(
        num_scalar_prefetch=1,
        grid=(bd, n_pages // pages),
        in_specs=[pl.BlockSpec((1, IDX_HEADS, IDX_DIM), lambda b, c, pt: (b, 0, 0)),
                  pl.BlockSpec((1, IDX_HEADS, 1), lambda b, c, pt: (b, 0, 0))] + [page_spec(i) for i in range(pages)],
        out_specs=pl.BlockSpec((1, 1, pages * PAGE_SIZE), lambda b, c, pt: (b, 0, c)),
    )
    return pl.pallas_call(
        functools.partial(_sscore_kernel, pages=pages),
        grid_spec=grid_spec,
        out_shape=jax.ShapeDtypeStruct((bd, 1, n_pages * PAGE_SIZE), F32),
        compiler_params=pltpu.CompilerParams(dimension_semantics=("arbitrary", "arbitrary")),
        name="sample_scores",
    )(page_table, qi, wi4, *([cache_ikT] * pages))


def _sample_select(scores, qi, ki_new, wi, n_past, cw, k_top):
    bd = scores.shape[0]
    width = n_past + LANES
    return pl.pallas_call(
        functools.partial(_sselect_kernel, n_past=n_past, cw=cw, k_top=k_top),
        out_shape=jax.ShapeDtypeStruct((bd, width), F32),
        scratch_shapes=[pltpu.VMEM((bd, width), jnp.int32)],
        compiler_params=pltpu.CompilerParams(vmem_limit_bytes=VMEM_LIMIT),
        name="sample_select",
    )(scores, qi, ki_new, wi)


def _sample_attn(page_table, qbd, madd, madd_new, sbias, sbias_new, k_new, v_new, cache_kT, cache_vT, pages):
    bd, n_pages = page_table.shape
    chunk = pages * PAGE_SIZE
    page_spec = lambda i: pl.BlockSpec((1, 1, N_KV_HEADS, HEAD_DIM, PAGE_SIZE),
                                       lambda b, c, pt: (0, pt[b, c * pages + i], 0, 0, 0))
    per_b = lambda shape: pl.BlockSpec((1,) + shape, lambda b, c, pt: (b, 0, 0))
    grid_spec = pltpu.PrefetchScalarGridSpec(
        num_scalar_prefetch=1,
        grid=(bd, n_pages // pages),
        in_specs=[per_b((N_HEADS, KV_WIDTH)),
                  pl.BlockSpec((1, 1, chunk), lambda b, c, pt: (b, 0, c)),
                  per_b((1, LANES)),
                  pl.BlockSpec((N_HEADS, chunk), lambda b, c, pt: (0, c)),
                  pl.BlockSpec((N_HEADS, LANES), lambda b, c, pt: (0, 0)),
                  per_b((1, KV_WIDTH)), per_b((1, KV_WIDTH))]
                 + [page_spec(i) for i in range(pages)] * 2,
        out_specs=per_b((N_HEADS, KV_WIDTH)),
        scratch_shapes=[pltpu.VMEM((N_HEADS, 1), F32), pltpu.VMEM((N_HEADS, 1), F32),
                        pltpu.VMEM((N_HEADS, KV_WIDTH), F32)],
    )
    return pl.pallas_call(
        functools.partial(_sattn_kernel, pages=pages),
        grid_spec=grid_spec,
        out_shape=jax.ShapeDtypeStruct((bd, N_HEADS, KV_WIDTH), F32),
        compiler_params=pltpu.CompilerParams(dimension_semantics=("arbitrary", "arbitrary"),
                                             vmem_limit_bytes=VMEM_LIMIT),
        name="sample_attn",
    )(page_table, qbd, madd, madd_new, sbias, sbias_new, k_new, v_new, *([cache_kT] * pages), *([cache_vT] * pages))


def _pick(n, prefs):
    for p in prefs:
        if n % p == 0:
            return p
    return n


def kernel(x_prompt, x_sample, cache_k, cache_v, cache_idx_k, state_pool, page_table, ln1, w_in, pool_w, pool_scale,
           w_pool_proj, w_attn_proj, w_out, ln2, w_gate_up, w_down, rel_bias, ln_final):
    assert w_in.shape[0] == 1, "one layer"
    nb, seq, _ = x_prompt.shape
    bd, dec_seq, _ = x_sample.shape
    assert dec_seq == 1 and seq % CHUNK == 0
    n_pages = page_table.shape[1]
    n_past = n_pages * PAGE_SIZE

    w_nat, w_t = _pack_w_in(w_in[0])
    ln1r, ln2r, lnfr = ln1[0][None], ln2[0][None], ln_final[None]
    pw = pool_w[0].astype(BF16)
    ps = pool_scale[0][None]
    wpp, wap, wout = w_pool_proj[0].astype(BF16), w_attn_proj[0].astype(BF16), w_out[0].astype(BF16)
    wgu, wd = w_gate_up[0].astype(BF16), w_down[0].astype(BF16)
    corr, brow, srow = _bias_tables(rel_bias)

    xp = x_prompt.reshape(nb * seq, D_MODEL)
    qp, kT, vT, vTa, kiT, wiT, kis, ka, mp, ga, ul = _proj_prompt(
        xp, ln1r, w_nat, w_t, pw, ps, wpp, nb, seq, _pick(seq, (512, 256, 128)))
    aaT = _attn_prompt(qp, wiT, kis, ka, vTa, corr, brow, nb, seq, min(TOPK_MAX, seq // 4))
    y_prompt = _merge_ffn(xp, mp, ga, aaT, wap, wout, ln2r, wgu, wd, lnfr, seq, _pick(seq, (256, 128)))
    y_prompt = y_prompt.reshape(nb, seq, D_MODEL)

    xs = x_sample.reshape(bd, D_MODEL)
    st = jnp.swapaxes(state_pool[0], 0, 1)
    qs, ks, vs, kin, qin, wis, mps, gas, us = _proj_sample(xs, ln1r, w_nat, w_t, pw, ps, wpp, st, n_past)
    pages = _pick(n_pages, (32, 16, 8, 4, 2))
    scores = _sample_scores(page_table, qin.reshape(bd, IDX_HEADS, IDX_DIM), wis[:, :IDX_HEADS, None],
                            jnp.swapaxes(cache_idx_k, 2, 3), _pick(n_pages, (64, 32, 16, 8, 4, 2))).reshape(bd, n_past)
    width = n_past + LANES
    madd = _sample_select(scores, qin, kin, wis, n_past, _pick(width, (640, 512, 384, 256, 128)),
                          min(TOPK_MAX, (n_past + 1) // 4))
    head_kv = np.arange(N_HEADS) // GROUP
    onehot = jnp.asarray(head_kv[:, None] == np.arange(N_KV_HEADS)[None, :])
    qbd = jnp.where(onehot[None, :, :, None], qs.reshape(bd, N_HEADS, 1, HEAD_DIM), 0).reshape(bd, N_HEADS, KV_WIDTH)
    far = jnp.broadcast_to(srow[:, 2 * LANES:2 * LANES + 1], (N_HEADS, n_past - LANES))
    sbias = jnp.concatenate([far, srow[:, :LANES]], axis=1)
    ao = _sample_attn(page_table, qbd, madd[:, None, :n_past], madd[:, None, n_past:], sbias, srow[:, LANES:2 * LANES],
                      ks[:, None, :], vs[:, None, :], jnp.transpose(cache_k, (0, 1, 3, 4, 2)),
                      jnp.transpose(cache_v, (0, 1, 3, 4, 2)), pages)
    aas = jnp.sum(jnp.where(onehot[None, :, :, None], ao.reshape(bd, N_HEADS, N_KV_HEADS, HEAD_DIM), 0), axis=2)
    aasT = aas.reshape(1, bd, ATTN_WIDTH).astype(BF16).transpose(0, 2, 1)[None]
    y_sample = _merge_ffn(xs, mps, gas, aasT, wap, wout, ln2r, wgu, wd, lnfr, bd, bd).reshape(bd, 1, D_MODEL)

    to_heads = lambda a: jnp.transpose(a.reshape(nb, N_KV_HEADS, HEAD_DIM, seq), (0, 3, 1, 2))[None]
    return (y_prompt, y_sample, to_heads(kT), to_heads(vT), jnp.swapaxes(kiT, 1, 2)[None], ul[None, :, 1:, :],
            ks.reshape(1, bd, 1, N_KV_HEADS, HEAD_DIM), vs.reshape(1, bd, 1, N_KV_HEADS, HEAD_DIM),
            kin.reshape(1, bd, 1, IDX_DIM),
            jnp.concatenate([state_pool[0][:, 1:], us[:, None, :]], axis=1)[None])
```

```python
import functools
import math

import jax
import jax.numpy as jnp
import numpy as np
from jax import lax
from jax.experimental import pallas as pl
from jax.experimental.pallas import tpu as pltpu

D_MODEL = 1024
PAGE_SIZE = 128
POOL_WIDTH = D_MODEL // 2
POOL_WINDOWS = (2, 4, 8, 16)
POOL_GROUP_DIM = POOL_WIDTH // len(POOL_WINDOWS)
POOL_BUF = 15
N_HEADS = 8
N_KV_HEADS = 4
HEAD_DIM = 64
GROUP = N_HEADS // N_KV_HEADS
ATTN_WIDTH = N_HEADS * HEAD_DIM
KV_WIDTH = N_KV_HEADS * HEAD_DIM
IDX_HEADS = 4
IDX_DIM = 64
TOPK_MAX = 256
N_BUCKETS = 32
MAX_DISTANCE = 128
D_FF = -(-(8 * D_MODEL) // (3 * 256)) * 256
EPS = 1e-6
IN_SPLITS = (POOL_WIDTH, ATTN_WIDTH, KV_WIDTH, KV_WIDTH, IDX_HEADS * IDX_DIM, IDX_DIM, IDX_HEADS, 2 * D_MODEL)

LANES = 128
SUBLANES = 8
QBLK = 256
CHUNK = 512
INT_MIN = -(2 ** 31)
NEG = -1e30
LOG2E = math.log2(math.e)
VMEM_LIMIT = 56 * 1024 * 1024
V_AUG = HEAD_DIM + 16

N_U, N_KA, N_KK, N_G = 0, 512, 1024, 1152
N_END = N_G + 2 * D_MODEL
T_Q, T_K, T_V, T_QI, T_KI, T_WI = 0, 512, 768, 1024, 1280, 1344
T_END = T_WI + SUBLANES
QP_ROWS = ATTN_WIDTH + IDX_HEADS * LANES

F32 = jnp.float32
BF16 = jnp.bfloat16


def _dot(a, b, precision=None):
    return jnp.dot(a, b, preferred_element_type=F32, precision=precision)


def _dot_nt(a, b):
    return lax.dot_general(a, b, (((1,), (1,)), ((), ())), preferred_element_type=F32)


def _dot_tn(a, b):
    return lax.dot_general(a, b, (((0,), (0,)), ((), ())), preferred_element_type=F32)


def _rms(x, g):
    return (x * lax.rsqrt(jnp.mean(x * x, axis=-1, keepdims=True) + EPS)) * g


def _hi_lo(z):
    hi = z.astype(BF16).astype(F32)
    return hi, z - hi


def _sortable_key(score):
    bits = lax.bitcast_convert_type(score, jnp.int32)
    return jnp.where(bits < 0, -(bits & 0x7FFFFFFF), bits)


def _gated_pool(d_groups, gates, pw_ref, ps_ref, wpp_ref):
    a = [_dot(d.astype(BF16), pw_ref[g]) for g, d in enumerate(d_groups)]
    a = jnp.concatenate(a, axis=1) * ps_ref[...]
    ap = _dot(a.astype(BF16), wpp_ref[...])
    return jax.nn.sigmoid(gates[:, :D_MODEL]) * ap


def _proj_kernel(x_ref, ln1_ref, wn_ref, wt_ref, pw_ref, ps_ref, wpp_ref,
                 qp_ref, kT_ref, vT_ref, vTa_ref, kiT_ref, wiT_ref, kis_ref, ka_ref, mp_ref, ga_ref, ul_ref,
                 uext_ref, *, tm, tiles_per_seq):
    ti = pl.program_id(0) % tiles_per_seq
    hb = _rms(x_ref[...], ln1_ref[...]).astype(BF16)

    def mm(lo, hi):
        return _dot(hb, wn_ref[:, lo:hi])

    def mt(lo, hi):
        return _dot_nt(wt_ref[lo:hi, :], hb)

    def per_qblk(ref, row0, val):
        for i in range(tm // QBLK):
            ref[0, i, row0:row0 + val.shape[0], :] = val[:, i * QBLK:(i + 1) * QBLK]

    per_qblk(qp_ref, 0, (mt(T_Q, T_K) * (HEAD_DIM ** -0.5 * LOG2E)).astype(BF16))
    kT_ref[0] = mt(T_K, T_V)
    vT = mt(T_V, T_QI)
    vT_ref[0] = vT
    ones = jnp.ones((V_AUG - HEAD_DIM, tm), F32)
    vTa_ref[0] = jnp.concatenate(
        [piece for c in range(N_KV_HEADS) for piece in (vT[c * HEAD_DIM:(c + 1) * HEAD_DIM], ones)], axis=0).astype(BF16)
    hi, lo = _hi_lo(mt(T_QI, T_KI))
    per_qblk(qp_ref, ATTN_WIDTH, jnp.concatenate(
        [piece for h in range(IDX_HEADS) for piece in (hi[h * IDX_DIM:(h + 1) * IDX_DIM], lo[h * IDX_DIM:(h + 1) * IDX_DIM])],
        axis=0).astype(BF16))
    kiT_ref[0] = mt(T_KI, T_WI)
    per_qblk(wiT_ref, 0, mt(T_WI, T_END) * (IDX_HEADS ** -0.5 * IDX_DIM ** -0.5))

    hi, lo = _hi_lo(mm(N_KK, N_G))
    kis_ref[...] = jnp.concatenate([hi, lo], axis=1).astype(BF16)
    lane = lax.broadcasted_iota(jnp.int32, (1, N_KV_HEADS * LANES), 1) % LANES
    bias_cols = jnp.where((lane == HEAD_DIM) | (lane == HEAD_DIM + 1), 1.0, 0.0)
    ka_ref[...] = (mm(N_KA, N_KK) + bias_cols).astype(BF16)

    u = mm(N_U, N_KA)

    @pl.when(ti == 0)
    def _():
        uext_ref[0:16, :] = jnp.zeros((16, POOL_WIDTH), F32)

    @pl.when(ti != 0)
    def _():
        uext_ref[0:16, :] = uext_ref[tm:tm + 16, :]

    uext_ref[16:16 + tm, :] = u
    ul_ref[0] = uext_ref[tm:tm + 16, :]
    pos1 = ti * tm + lax.broadcasted_iota(jnp.int32, (tm, 1), 0) + 1
    ds = []
    for g, w in enumerate(POOL_WINDOWS):
        sl = slice(g * POOL_GROUP_DIM, (g + 1) * POOL_GROUP_DIM)
        win = uext_ref[16:16 + tm, sl]
        for d in range(1, w):
            win = win + uext_ref[16 - d:16 - d + tm, sl]
        cnt = jnp.minimum(w, pos1).astype(F32)
        ds.append(win / cnt - u[:, sl])
    gates = mm(N_G, N_END)
    mp_ref[...] = _gated_pool(ds, gates, pw_ref, ps_ref, wpp_ref)
    ga_ref[...] = jax.nn.sigmoid(gates[:, D_MODEL:])


def _sproj_kernel(x_ref, ln1_ref, wn_ref, wt_ref, pw_ref, ps_ref, wpp_ref, st_ref,
                  q_ref, k_ref, v_ref, ki_ref, qi_ref, wi_ref, mp_ref, ga_ref, u_ref, *, n_past):
    hb = _rms(x_ref[...], ln1_ref[...]).astype(BF16)

    def mm(lo, hi):
        return _dot(hb, wn_ref[:, lo:hi])

    def mn(lo, hi):
        return _dot_nt(hb, wt_ref[lo:hi, :])

    q_ref[...] = (mn(T_Q, T_K) * HEAD_DIM ** -0.5).astype(BF16)
    k_ref[...] = mn(T_K, T_V)
    v_ref[...] = mn(T_V, T_QI)
    qi_ref[...] = mn(T_QI, T_KI)
    ki_ref[...] = mn(T_KI, T_WI)
    wi_ref[...] = mn(T_WI, T_END) * IDX_HEADS ** -0.5
    u = mm(N_U, N_KA)
    u_ref[...] = u
    ds = []
    for g, w in enumerate(POOL_WINDOWS):
        sl = slice(g * POOL_GROUP_DIM, (g + 1) * POOL_GROUP_DIM)
        win = u[:, sl]
        for d in range(1, w):
            win = win + st_ref[POOL_BUF - d][:, sl]
        ds.append(win / float(min(w, n_past + 1)) - u[:, sl])
    gates = mm(N_G, N_END)
    mp_ref[...] = _gated_pool(ds, gates, pw_ref, ps_ref, wpp_ref)
    ga_ref[...] = jax.nn.sigmoid(gates[:, D_MODEL:])


def _bias_kernel(rb_ref, corr_ref, brow_ref, srow_ref):
    i = lax.broadcasted_iota(jnp.int32, (LANES, LANES), 0)
    j = lax.broadcasted_iota(jnp.int32, (LANES, LANES), 1)
    max_exact = N_BUCKETS // 2

    def bias_of(dist, h):
        d = jnp.maximum(dist, 1).astype(F32)
        large = max_exact + (jnp.log(d / max_exact) / math.log(MAX_DISTANCE / max_exact)
                             * (N_BUCKETS - max_exact)).astype(jnp.int32)
        bucket = jnp.where(dist < max_exact, dist, jnp.minimum(large, N_BUCKETS - 1))
        acc = jnp.zeros(dist.shape, F32)
        for b in range(N_BUCKETS):
            acc = jnp.where(bucket == b, rb_ref[b, h], acc)
        return acc

    row = lax.broadcasted_iota(jnp.int32, (HEAD_DIM, QBLK), 0)
    lane1 = lax.broadcasted_iota(jnp.int32, (1, LANES), 1)
    for h in range(N_HEADS):
        far = rb_ref[N_BUCKETS - 1, h]
        for t in range(2):
            corr_ref[t, h] = (bias_of(jnp.maximum(t * LANES + j - i, 0), h) - far) * LOG2E
        corr_ref[2, h] = jnp.zeros((LANES, LANES), F32)
        far_v = jnp.full((HEAD_DIM, QBLK), far, F32) * LOG2E
        hi, lo = _hi_lo(far_v)
        brow_ref[h] = jnp.where(row == 0, hi, jnp.where(row == 1, lo, 0.0))
        srow_ref[h:h + 1, :] = jnp.concatenate([bias_of(LANES - lane1, h), bias_of(jnp.zeros((1, LANES), jnp.int32), h),
                                       jnp.full((1, LANES), far, F32)], axis=1)


def _topk_mask_cols(keys_ref, madd_ref, nc, cw, k_top):
    part = 4 * SUBLANES
    width = keys_ref.shape[1]

    def count_ge(thr):
        def body(c, acc):
            kk = keys_ref[pl.ds(pl.multiple_of(c * cw, cw), cw), :]
            for i in range(cw // part):
                acc = jnp.where(kk[i * part:(i + 1) * part] >= thr, acc + 1, acc)
            return acc

        acc = lax.fori_loop(0, nc, body, jnp.zeros((part, width), jnp.int32))
        return jnp.sum(acc, axis=0, keepdims=True)

    def bit_step(it, thr):
        cand = thr + lax.shift_left(jnp.int32(1), 31 - it)
        return jnp.where(count_ge(cand) >= k_top, cand, thr)

    thr = lax.fori_loop(0, 32, bit_step, jnp.full((1, width), INT_MIN, jnp.int32))
    thr_eff = jnp.maximum(thr, INT_MIN + 1)

    def mask_body(c, acc):
        off = pl.multiple_of(c * cw, cw)
        sel = keys_ref[pl.ds(off, cw), :] >= thr_eff
        madd_ref[pl.ds(off, cw), :] = jnp.where(sel, 0.0, NEG)
        return acc + jnp.sum(jnp.where(sel, 1, 0).reshape(cw // part, part, width), axis=0)

    acc = lax.fori_loop(0, nc, mask_body, jnp.zeros((part, width), jnp.int32))
    n_sel = jnp.sum(acc, axis=0, keepdims=True)

    @pl.when(jnp.max(n_sel) > k_top)
    def _():
        room = (k_top - count_ge(thr + 1)).astype(F32)
        a = lax.broadcasted_iota(jnp.int32, (LANES, LANES), 0)
        b = lax.broadcasted_iota(jnp.int32, (LANES, LANES), 1)
        lower = jnp.where(b <= a, 1.0, 0.0).astype(BF16)

        def tie_body(c, seen):
            off = pl.multiple_of(c * LANES, LANES)
            kk = keys_ref[pl.ds(off, LANES), :]
            eq = kk == thr
            ones = jnp.where(eq, 1.0, 0.0)
            rank = _dot(lower, ones.astype(BF16)) + seen
            tied = jnp.where(rank <= room, 0.0, NEG)
            tied = jnp.where(kk > INT_MIN, tied, NEG)
            madd_ref[pl.ds(off, LANES), :] = jnp.where(eq, tied, jnp.where(kk > thr, 0.0, NEG))
            return seen + jnp.sum(ones, axis=0, keepdims=True)

        lax.fori_loop(0, nc * (cw // LANES), tie_body, jnp.zeros((1, width), F32))


def _topk_mask_rows(keys_ref, madd_ref, nc, cw, k_top):
    rows = keys_ref.shape[0]
    sub = cw // LANES

    def count_ge(thr):
        thr_b = jnp.broadcast_to(thr, (rows, LANES))

        def body(c, acc):
            off = pl.multiple_of(c * cw, LANES)
            for s in range(sub):
                kk = keys_ref[:, pl.ds(off + s * LANES, LANES)]
                acc = acc + jnp.where(kk >= thr_b, 1, 0)
            return acc

        acc = lax.fori_loop(0, nc, body, jnp.zeros((rows, LANES), jnp.int32))
        return jnp.sum(acc, axis=1, keepdims=True)

    def bit_step(it, thr):
        cand = thr + lax.shift_left(jnp.int32(1), 31 - it)
        return jnp.where(count_ge(cand) >= k_top, cand, thr)

    thr = lax.fori_loop(0, 32, bit_step, jnp.full((rows, 1), INT_MIN, jnp.int32))
    thr_b = jnp.broadcast_to(jnp.maximum(thr, INT_MIN + 1), (rows, LANES))

    def mask_body(c, acc):
        off = pl.multiple_of(c * cw, LANES)
        for s in range(sub):
            kk = keys_ref[:, pl.ds(off + s * LANES, LANES)]
            sel = kk >= thr_b
            madd_ref[:, pl.ds(off + s * LANES, LANES)] = jnp.where(sel, 0.0, NEG)
            acc = acc + jnp.where(sel, 1, 0)
        return acc

    acc = lax.fori_loop(0, nc, mask_body, jnp.zeros((rows, LANES), jnp.int32))
    n_sel = jnp.sum(acc, axis=1, keepdims=True)

    @pl.when(jnp.max(n_sel) > k_top)
    def _():
        room = (k_top - count_ge(thr + 1)).astype(F32)
        a = lax.broadcasted_iota(jnp.int32, (LANES, LANES), 0)
        b = lax.broadcasted_iota(jnp.int32, (LANES, LANES), 1)
        upper = jnp.where(a <= b, 1.0, 0.0).astype(BF16)
        eq_thr = jnp.broadcast_to(thr, (rows, LANES))

        def tie_body(c, seen):
            off = pl.multiple_of(c * LANES, LANES)
            kk = keys_ref[:, pl.ds(off, LANES)]
            eq = kk == eq_thr
            ones = jnp.where(eq, 1.0, 0.0)
            rank = _dot(ones.astype(BF16), upper) + seen
            tied = jnp.where(rank <= room, 0.0, NEG)
            tied = jnp.where(kk > INT_MIN, tied, NEG)
            madd_ref[:, pl.ds(off, LANES)] = jnp.where(eq, tied, jnp.where(kk > eq_thr, 0.0, NEG))
            return seen + jnp.sum(ones, axis=1, keepdims=True)

        lax.fori_loop(0, nc * sub, tie_body, jnp.zeros((rows, 1), F32))


def _attn_kernel(qp_ref, wiT_ref, kis_ref, ka_ref, vTa_ref, corr_ref, brow_ref, o_ref,
                 keys_ref, madd_ref, m_ref, acc_ref, s0_ref, s1_ref, cm0_ref, cm1_ref, p_ref, qs_ref, qa_ref, *, k_top):
    j = pl.program_id(1)
    t0 = j * QBLK
    cj = t0 // CHUNK
    nc = cj + 1
    qpos = t0 + lax.broadcasted_iota(jnp.int32, (1, QBLK), 1)

    for h in range(IDX_HEADS):
        rows = slice(ATTN_WIDTH + h * LANES, ATTN_WIDTH + (h + 1) * LANES)
        qs_ref[h] = jnp.concatenate([qp_ref[0, 0, rows, :]] * 2, axis=0)

    def score_body(c, carry):
        off = pl.multiple_of(c * CHUNK, CHUNK)
        sc = jnp.zeros((CHUNK, QBLK), F32)
        for h in range(IDX_HEADS):
            s = _dot(kis_ref[pl.ds(off, CHUNK), :], qs_ref[h])
            sc = sc + wiT_ref[0, 0, h:h + 1, :] * jnp.maximum(s, 0.0)
        kpos = off + lax.broadcasted_iota(jnp.int32, (CHUNK, 1), 0)
        keys_ref[pl.ds(off, CHUNK), :] = jnp.where(kpos <= qpos, _sortable_key(sc), INT_MIN)
        return carry

    lax.fori_loop(0, nc, score_body, 0)
    _topk_mask_cols(keys_ref, madd_ref, nc, CHUNK, k_top)

    for h in range(N_HEADS):
        qa_ref[h] = jnp.concatenate([qp_ref[0, 0, h * HEAD_DIM:(h + 1) * HEAD_DIM, :], brow_ref[h].astype(BF16)], axis=0)
    m_ref[...] = jnp.full(m_ref.shape, NEG, F32)
    acc_ref[...] = jnp.zeros(acc_ref.shape, F32)

    nblk = CHUNK // LANES
    nsub = QBLK // LANES
    s_bufs, cm_bufs = (s0_ref, s1_ref), (cm0_ref, cm1_ref)

    def logits(c, near, s_ref, cm_ref):
        off = pl.multiple_of(c * CHUNK, CHUNK)
        for h in range(N_HEADS):
            g = h // GROUP
            mx = jnp.full((SUBLANES, QBLK), NEG, F32)
            for i in range(nblk):
                rows = pl.ds(off + i * LANES, LANES)
                s = _dot(ka_ref[rows, g * LANES:(g + 1) * LANES], qa_ref[h]) + madd_ref[rows, :]
                if near:
                    tiles = [corr_ref[jnp.clip(j * nsub + q - (c * nblk + i), 0, 2), h] for q in range(nsub)]
                    s = s + jnp.concatenate(tiles, axis=1)
                s_ref[h, i * LANES:(i + 1) * LANES, :] = s
                mx = jnp.maximum(mx, jnp.max(s.reshape(LANES // SUBLANES, SUBLANES, QBLK), axis=0))
            cm_ref[h] = jnp.max(mx, axis=0, keepdims=True)

    def accumulate(c, s_ref, cm_ref):
        off = pl.multiple_of(c * CHUNK, CHUNK)
        for h in range(N_HEADS):
            g = h // GROUP
            m_prev = m_ref[h]
            m_next = jnp.maximum(m_prev, cm_ref[h])
            for i in range(nblk):
                blk = slice(i * LANES, (i + 1) * LANES)
                p_ref[h, blk, :] = jnp.exp2(s_ref[h, blk, :] - m_next).astype(BF16)
            pv = _dot(vTa_ref[0, g * V_AUG:(g + 1) * V_AUG, pl.ds(off, CHUNK)], p_ref[h])
            acc_ref[h] = jnp.exp2(m_prev - m_next) * acc_ref[h] + pv
            m_ref[h] = m_next

    def step(c, near):
        for par in (0, 1):
            @pl.when(c % 2 == par)
            def _(par=par):
                logits(c + 1, near, s_bufs[1 - par], cm_bufs[1 - par])
                accumulate(c, s_bufs[par], cm_bufs[par])

    def far_body(c, carry):
        step(c, False)
        return carry

    def near_body(c, carry):
        step(c, True)
        return carry

    n_far = jnp.maximum(cj - 2, 0)
    logits(0, True, s_bufs[0], cm_bufs[0])
    lax.fori_loop(0, n_far, far_body, 0)
    lax.fori_loop(n_far, cj, near_body, 0)
    for par in (0, 1):
        @pl.when(cj % 2 == par)
        def _(par=par):
            accumulate(cj, s_bufs[par], cm_bufs[par])

    for h in range(N_HEADS):
        acc = acc_ref[h]
        o_ref[0, 0, h * HEAD_DIM:(h + 1) * HEAD_DIM, :] = (acc[0:HEAD_DIM] / acc[HEAD_DIM:HEAD_DIM + 1]).astype(o_ref.dtype)


def _ffn_kernel(x_ref, mp_ref, ga_ref, aaT_ref, wap_ref, wout_ref, ln2_ref, wgu_ref, wd_ref, lnf_ref, y_ref, *, fc):
    aaT = jnp.concatenate([aaT_ref[0, i] for i in range(aaT_ref.shape[1])], axis=1)
    att = _dot_tn(aaT, wap_ref[...])
    m = mp_ref[...] + ga_ref[...] * att
    h = x_ref[...] + _dot(m.astype(BF16), wout_ref[...])
    hn = _rms(h, ln2_ref[...]).astype(BF16)
    y = h
    for c in range(D_FF // fc):
        gate = _dot(hn, wgu_ref[:, c * fc:(c + 1) * fc])
        up = _dot(hn, wgu_ref[:, D_FF + c * fc:D_FF + (c + 1) * fc])
        act = (gate * jax.nn.sigmoid(gate)) * up
        y = y + _dot(act.astype(BF16), wd_ref[c * fc:(c + 1) * fc, :])
    y_ref[...] = _rms(y, lnf_ref[...])


def _sscore_kernel(pt_ref, qi_ref, wi_ref, *refs, pages):
    page_refs, o_ref = refs[:pages], refs[pages]
    kc = jnp.concatenate([r[0, 0] for r in page_refs], axis=1)
    s = _dot(qi_ref[0], kc, precision=lax.Precision.HIGHEST) * IDX_DIM ** -0.5
    o_ref[0] = jnp.sum(wi_ref[0] * jnp.maximum(s, 0.0), axis=0, keepdims=True)


def _sselect_kernel(sc_ref, qi_ref, kin_ref, wi_ref, madd_ref, keys_ref, *, n_past, cw, k_top):
    rows = sc_ref.shape[0]
    keys_ref[:, 0:n_past] = _sortable_key(sc_ref[...])
    kin = kin_ref[...]
    wi = wi_ref[...]
    s_new = jnp.zeros((rows, 1), F32)
    for h in range(IDX_HEADS):
        s = jnp.sum(qi_ref[:, h * IDX_DIM:(h + 1) * IDX_DIM] * kin, axis=1, keepdims=True) * IDX_DIM ** -0.5
        s_new = s_new + wi[:, h:h + 1] * jnp.maximum(s, 0.0)
    lane = lax.broadcasted_iota(jnp.int32, (rows, keys_ref.shape[1] - n_past), 1)
    keys_ref[:, n_past:] = jnp.where(lane == 0, _sortable_key(s_new), INT_MIN)
    _topk_mask_rows(keys_ref, madd_ref, keys_ref.shape[1] // cw, cw, k_top)


def _sattn_kernel(pt_ref, qbd_ref, madd_ref, maddn_ref, sb_ref, sbn_ref, kn_ref, vn_ref, *refs, pages):
    k_refs, v_refs = refs[:pages], refs[pages:2 * pages]
    o_ref, m_ref, l_ref, acc_ref = refs[2 * pages:]
    c = pl.program_id(1)

    @pl.when(c == 0)
    def _():
        m_ref[...] = jnp.full(m_ref.shape, NEG, F32)
        l_ref[...] = jnp.zeros(l_ref.shape, F32)
        acc_ref[...] = jnp.zeros(acc_ref.shape, F32)

    qbd = qbd_ref[0]
    kc = jnp.concatenate([r[0, 0].reshape(KV_WIDTH, PAGE_SIZE) for r in k_refs], axis=1).astype(BF16)
    vc = jnp.concatenate([r[0, 0].reshape(KV_WIDTH, PAGE_SIZE) for r in v_refs], axis=1).astype(BF16)
    s = _dot(qbd, kc) + sb_ref[...] + madd_ref[0]
    m_prev = m_ref[...]
    m_next = jnp.maximum(m_prev, jnp.max(s, axis=1, keepdims=True))
    alpha = jnp.exp(m_prev - m_next)
    p = jnp.exp(s - m_next)
    l_ref[...] = alpha * l_ref[...] + jnp.sum(p, axis=1, keepdims=True)
    acc_ref[...] = alpha * acc_ref[...] + _dot_nt(p.astype(BF16), vc)
    m_ref[...] = m_next

    @pl.when(c == pl.num_programs(1) - 1)
    def _():
        s_new = (jnp.sum(qbd.astype(F32) * kn_ref[0], axis=1, keepdims=True)
                 + sbn_ref[:, 0:1] + maddn_ref[0][:, 0:1])
        m_prev = m_ref[...]
        m_next = jnp.maximum(m_prev, s_new)
        alpha = jnp.exp(m_prev - m_next)
        p = jnp.exp(s_new - m_next)
        l = alpha * l_ref[...] + p
        o_ref[0] = (alpha * acc_ref[...] + p * vn_ref[0]) / l


def _full(shape):
    return pl.BlockSpec(shape, lambda *_: (0,) * len(shape))


def _pack_w_in(w):
    offs = np.cumsum((0,) + IN_SPLITS)
    wu, wq, wk, wv, wqi, wki, wwi, wg = [w[:, offs[i]:offs[i + 1]] for i in range(len(IN_SPLITS))]
    pad = jnp.zeros((D_MODEL, LANES - HEAD_DIM), w.dtype)
    wka = jnp.concatenate([piece for c in range(N_KV_HEADS) for piece in (wk[:, c * HEAD_DIM:(c + 1) * HEAD_DIM], pad)], axis=1)
    w_nat = jnp.concatenate([wu, wka, wki, wki, wg], axis=1).astype(BF16)
    wwp = jnp.pad(wwi, ((0, 0), (0, SUBLANES - IDX_HEADS)))
    w_t = jnp.concatenate([wq, wk, wv, wqi, wki, wwp], axis=1).T.astype(BF16)
    return w_nat, w_t


def _proj_prompt(x2, ln1, w_nat, w_t, pw, ps, wpp, nb, seq, tm):
    t = x2.shape[0]
    tps = seq // tm
    row = lambda n: pl.BlockSpec((tm, n), lambda i: (i, 0))
    col = lambda n: pl.BlockSpec((1, n, tm), lambda i: (i // tps, 0, i % tps))
    tshape = lambda n, dt: jax.ShapeDtypeStruct((nb, n, seq), dt)
    qblk = lambda n: pl.BlockSpec((1, tm // QBLK, n, QBLK), lambda i: (i // tps, i % tps, 0, 0))
    out_shape = (
        jax.ShapeDtypeStruct((nb, seq // QBLK, QP_ROWS, QBLK), BF16),
        tshape(KV_WIDTH, F32), tshape(KV_WIDTH, F32),
        tshape(N_KV_HEADS * V_AUG, BF16),
        tshape(IDX_DIM, F32),
        jax.ShapeDtypeStruct((nb, seq // QBLK, SUBLANES, QBLK), F32),
        jax.ShapeDtypeStruct((t, 2 * LANES), BF16),
        jax.ShapeDtypeStruct((t, N_KV_HEADS * LANES), BF16),
        jax.ShapeDtypeStruct((t, D_MODEL), F32),
        jax.ShapeDtypeStruct((t, D_MODEL), F32),
        jax.ShapeDtypeStruct((nb, 16, POOL_WIDTH), F32),
    )
    out_specs = (qblk(QP_ROWS), col(KV_WIDTH), col(KV_WIDTH), col(N_KV_HEADS * V_AUG), col(IDX_DIM),
                 qblk(SUBLANES), row(2 * LANES), row(N_KV_HEADS * LANES), row(D_MODEL),
                 row(D_MODEL), pl.BlockSpec((1, 16, POOL_WIDTH), lambda i: (i // tps, 0, 0)))
    return pl.pallas_call(
        functools.partial(_proj_kernel, tm=tm, tiles_per_seq=tps),
        grid=(t // tm,),
        in_specs=[row(D_MODEL), _full((1, D_MODEL)), _full(w_nat.shape), _full(w_t.shape), _full(pw.shape),
                  _full((1, POOL_WIDTH)), _full(wpp.shape)],
        out_specs=out_specs,
        out_shape=out_shape,
        scratch_shapes=[pltpu.VMEM((tm + 16, POOL_WIDTH), F32)],
        compiler_params=pltpu.CompilerParams(dimension_semantics=("arbitrary",), vmem_limit_bytes=VMEM_LIMIT),
        name="proj_pool",
    )(x2, ln1, w_nat, w_t, pw, ps, wpp)


def _proj_sample(x2, ln1, w_nat, w_t, pw, ps, wpp, st, n_past):
    m = x2.shape[0]
    shapes = [(ATTN_WIDTH, BF16), (KV_WIDTH, F32), (KV_WIDTH, F32), (IDX_DIM, F32), (IDX_HEADS * IDX_DIM, F32),
              (SUBLANES, F32), (D_MODEL, F32), (D_MODEL, F32), (POOL_WIDTH, F32)]
    return pl.pallas_call(
        functools.partial(_sproj_kernel, n_past=n_past),
        out_shape=tuple(jax.ShapeDtypeStruct((m, n), dt) for n, dt in shapes),
        compiler_params=pltpu.CompilerParams(vmem_limit_bytes=VMEM_LIMIT),
        name="proj_pool_sample",
    )(x2, ln1, w_nat, w_t, pw, ps, wpp, st)


def _bias_tables(rel_bias):
    return pl.pallas_call(
        _bias_kernel,
        in_specs=[pl.BlockSpec(memory_space=pltpu.SMEM)],
        out_shape=(jax.ShapeDtypeStruct((3, N_HEADS, LANES, LANES), F32),
                   jax.ShapeDtypeStruct((N_HEADS, HEAD_DIM, QBLK), F32),
                   jax.ShapeDtypeStruct((N_HEADS, 3 * LANES), F32)),
        name="bias_tables",
    )(rel_bias)


def _attn_prompt(qp, wiT, kis, ka, vTa, corr, brow, nb, seq, k_top):
    nq = seq // QBLK
    qblk = lambda n: pl.BlockSpec((1, 1, n, QBLK), lambda b, j: (b, j, 0, 0))
    seqrow = lambda n: pl.BlockSpec((seq, n), lambda b, j: (b, 0))
    return pl.pallas_call(
        functools.partial(_attn_kernel, k_top=k_top),
        grid=(nb, nq),
        in_specs=[qblk(QP_ROWS), qblk(SUBLANES), seqrow(2 * LANES), seqrow(N_KV_HEADS * LANES),
                  pl.BlockSpec((1, N_KV_HEADS * V_AUG, seq), lambda b, j: (b, 0, 0)),
                  _full(corr.shape), _full(brow.shape)],
        out_specs=qblk(ATTN_WIDTH),
        out_shape=jax.ShapeDtypeStruct((nb, nq, ATTN_WIDTH, QBLK), BF16),
        scratch_shapes=[pltpu.VMEM((seq, QBLK), jnp.int32), pltpu.VMEM((seq, QBLK), F32),
                        pltpu.VMEM((N_HEADS, 1, QBLK), F32),
                        pltpu.VMEM((N_HEADS, V_AUG, QBLK), F32),
                        pltpu.VMEM((N_HEADS, CHUNK, QBLK), F32),
                        pltpu.VMEM((N_HEADS, CHUNK, QBLK), F32),
                        pltpu.VMEM((N_HEADS, 1, QBLK), F32),
                        pltpu.VMEM((N_HEADS, 1, QBLK), F32),
                        pltpu.VMEM((N_HEADS, CHUNK, QBLK), BF16),
                        pltpu.VMEM((IDX_HEADS, 4 * IDX_DIM, QBLK), BF16),
                        pltpu.VMEM((N_HEADS, LANES, QBLK), BF16)],
        compiler_params=pltpu.CompilerParams(dimension_semantics=("arbitrary", "arbitrary"),
                                             vmem_limit_bytes=VMEM_LIMIT),
        name="sparse_attn",
    )(qp, wiT, kis, ka, vTa, corr, brow)


def _merge_ffn(x2, mp, ga, aaT, wap, wout, ln2, wgu, wd, lnf, seq, tm):
    qb = aaT.shape[-1]
    t = x2.shape[0]
    tps = seq // tm
    row = lambda n: pl.BlockSpec((tm, n), lambda i: (i, 0))
    return pl.pallas_call(
        functools.partial(_ffn_kernel, fc=256),
        grid=(t // tm,),
        in_specs=[row(D_MODEL), row(D_MODEL), row(D_MODEL),
                  pl.BlockSpec((1, tm // qb, ATTN_WIDTH, qb), lambda i: (i // tps, i % tps, 0, 0)),
                  _full(wap.shape), _full(wout.shape), _full((1, D_MODEL)), _full(wgu.shape), _full(wd.shape),
                  _full((1, D_MODEL))],
        out_specs=row(D_MODEL),
        out_shape=jax.ShapeDtypeStruct((t, D_MODEL), F32),
        compiler_params=pltpu.CompilerParams(dimension_semantics=("arbitrary",), vmem_limit_bytes=VMEM_LIMIT),
        name="merge_ffn",
    )(x2, mp, ga, aaT, wap, wout, ln2, wgu, wd, lnf)


def _sample_scores(page_table, qi, wi4, cache_ikT, pages):
    bd, n_pages = page_table.shape
    page_spec = lambda i: pl.BlockSpec((1, 1, IDX_DIM, PAGE_SIZE), lambda b, c, pt: (0, pt[b, c * pages + i], 0, 0))
    grid_spec = pltpu.PrefetchScalarGridSpec(
        num_scalar_prefetch=1,
        grid=(bd, n_pages // pages),
        in_specs=[pl.BlockSpec((1, IDX_HEADS, IDX_DIM), lambda b, c, pt: (b, 0, 0)),
                  pl.BlockSpec((1, IDX_HEADS, 1), lambda b, c, pt: (b, 0, 0))] + [page_spec(i) for i in range(pages)],
        out_specs=pl.BlockSpec((1, 1, pages * PAGE_SIZE), lambda b, c, pt: (b, 0, c)),
    )
    return pl.pallas_call(
        functools.partial(_sscore_kernel, pages=pages),
        grid_spec=grid_spec,
        out_shape=jax.ShapeDtypeStruct((bd, 1, n_pages * PAGE_SIZE), F32),
        compiler_params=pltpu.CompilerParams(dimension_semantics=("arbitrary", "arbitrary")),
        name="sample_scores",
    )(page_table, qi, wi4, *([cache_ikT] * pages))


def _sample_select(scores, qi, ki_new, wi, n_past, cw, k_top):
    bd = scores.shape[0]
    width = n_past + LANES
    return pl.pallas_call(
        functools.partial(_sselect_kernel, n_past=n_past, cw=cw, k_top=k_top),
        out_shape=jax.ShapeDtypeStruct((bd, width), F32),
        scratch_shapes=[pltpu.VMEM((bd, width), jnp.int32)],
        compiler_params=pltpu.CompilerParams(vmem_limit_bytes=VMEM_LIMIT),
        name="sample_select",
    )(scores, qi, ki_new, wi)


def _sample_attn(page_table, qbd, madd, madd_new, sbias, sbias_new, k_new, v_new, cache_kT, cache_vT, pages):
    bd, n_pages = page_table.shape
    chunk = pages * PAGE_SIZE
    page_spec = lambda i: pl.BlockSpec((1, 1, N_KV_HEADS, HEAD_DIM, PAGE_SIZE),
                                       lambda b, c, pt: (0, pt[b, c * pages + i], 0, 0, 0))
    per_b = lambda shape: pl.BlockSpec((1,) + shape, lambda b, c, pt: (b, 0, 0))
    grid_spec = pltpu.PrefetchScalarGridSpec(
        num_scalar_prefetch=1,
        grid=(bd, n_pages // pages),
        in_specs=[per_b((N_HEADS, KV_WIDTH)),
                  pl.BlockSpec((1, 1, chunk), lambda b, c, pt: (b, 0, c)),
                  per_b((1, LANES)),
                  pl.BlockSpec((N_HEADS, chunk), lambda b, c, pt: (0, c)),
                  pl.BlockSpec((N_HEADS, LANES), lambda b, c, pt: (0, 0)),
                  per_b((1, KV_WIDTH)), per_b((1, KV_WIDTH))]
                 + [page_spec(i) for i in range(pages)] * 2,
        out_specs=per_b((N_HEADS, KV_WIDTH)),
        scratch_shapes=[pltpu.VMEM((N_HEADS, 1), F32), pltpu.VMEM((N_HEADS, 1), F32),
                        pltpu.VMEM((N_HEADS, KV_WIDTH), F32)],
    )
    return pl.pallas_call(
        functools.partial(_sattn_kernel, pages=pages),
        grid_spec=grid_spec,
        out_shape=jax.ShapeDtypeStruct((bd, N_HEADS, KV_WIDTH), F32),
        compiler_params=pltpu.CompilerParams(dimension_semantics=("arbitrary", "arbitrary"),
                                             vmem_limit_bytes=VMEM_LIMIT),
        name="sample_attn",
    )(page_table, qbd, madd, madd_new, sbias, sbias_new, k_new, v_new, *([cache_kT] * pages), *([cache_vT] * pages))


def _pick(n, prefs):
    for p in prefs:
        if n % p == 0:
            return p
    return n


def kernel(x_prompt, x_sample, cache_k, cache_v, cache_idx_k, state_pool, page_table, ln1, w_in, pool_w, pool_scale,
           w_pool_proj, w_attn_proj, w_out, ln2, w_gate_up, w_down, rel_bias, ln_final):
    assert w_in.shape[0] == 1, "one layer"
    nb, seq, _ = x_prompt.shape
    bd, dec_seq, _ = x_sample.shape
    assert dec_seq == 1 and seq % CHUNK == 0
    n_pages = page_table.shape[1]
    n_past = n_pages * PAGE_SIZE

    w_nat, w_t = _pack_w_in(w_in[0])
    ln1r, ln2r, lnfr = ln1[0][None], ln2[0][None], ln_final[None]
    pw = pool_w[0].astype(BF16)
    ps = pool_scale[0][None]
    wpp, wap, wout = w_pool_proj[0].astype(BF16), w_attn_proj[0].astype(BF16), w_out[0].astype(BF16)
    wgu, wd = w_gate_up[0].astype(BF16), w_down[0].astype(BF16)
    corr, brow, srow = _bias_tables(rel_bias)

    xp = x_prompt.reshape(nb * seq, D_MODEL)
    qp, kT, vT, vTa, kiT, wiT, kis, ka, mp, ga, ul = _proj_prompt(
        xp, ln1r, w_nat, w_t, pw, ps, wpp, nb, seq, _pick(seq, (512, 256, 128)))
    aaT = _attn_prompt(qp, wiT, kis, ka, vTa, corr, brow, nb, seq, min(TOPK_MAX, seq // 4))
    y_prompt = _merge_ffn(xp, mp, ga, aaT, wap, wout, ln2r, wgu, wd, lnfr, seq, _pick(seq, (512, 256)))
    y_prompt = y_prompt.reshape(nb, seq, D_MODEL)

    xs = x_sample.reshape(bd, D_MODEL)
    st = jnp.swapaxes(state_pool[0], 0, 1)
    qs, ks, vs, kin, qin, wis, mps, gas, us = _proj_sample(xs, ln1r, w_nat, w_t, pw, ps, wpp, st, n_past)
    pages = _pick(n_pages, (32, 16, 8, 4, 2))
    scores = _sample_scores(page_table, qin.reshape(bd, IDX_HEADS, IDX_DIM), wis[:, :IDX_HEADS, None],
                            jnp.swapaxes(cache_idx_k, 2, 3), _pick(n_pages, (64, 32, 16, 8, 4, 2))).reshape(bd, n_past)
    width = n_past + LANES
    madd = _sample_select(scores, qin, kin, wis, n_past, _pick(width, (640, 512, 384, 256, 128)),
                          min(TOPK_MAX, (n_past + 1) // 4))
    head_kv = np.arange(N_HEADS) // GROUP
    onehot = jnp.asarray(head_kv[:, None] == np.arange(N_KV_HEADS)[None, :])
    qbd = jnp.where(onehot[None, :, :, None], qs.reshape(bd, N_HEADS, 1, HEAD_DIM), 0).reshape(bd, N_HEADS, KV_WIDTH)
    far = jnp.broadcast_to(srow[:, 2 * LANES:2 * LANES + 1], (N_HEADS, n_past - LANES))
    sbias = jnp.concatenate([far, srow[:, :LANES]], axis=1)
    ao = _sample_attn(page_table, qbd, madd[:, None, :n_past], madd[:, None, n_past:], sbias, srow[:, LANES:2 * LANES],
                      ks[:, None, :], vs[:, None, :], jnp.transpose(cache_k, (0, 1, 3, 4, 2)),
                      jnp.transpose(cache_v, (0, 1, 3, 4, 2)), pages)
    aas = jnp.sum(jnp.where(onehot[None, :, :, None], ao.reshape(bd, N_HEADS, N_KV_HEADS, HEAD_DIM), 0), axis=2)
    aasT = aas.reshape(1, bd, ATTN_WIDTH).astype(BF16).transpose(0, 2, 1)[None]
    y_sample = _merge_ffn(xs, mps, gas, aasT, wap, wout, ln2r, wgu, wd, lnfr, bd, bd).reshape(bd, 1, D_MODEL)

    to_heads = lambda a: jnp.transpose(a.reshape(nb, N_KV_HEADS, HEAD_DIM, seq), (0, 3, 1, 2))[None]
    return (y_prompt, y_sample, to_heads(kT), to_heads(vT), jnp.swapaxes(kiT, 1, 2)[None], ul[None, :, 1:, :],
            ks.reshape(1, bd, 1, N_KV_HEADS, HEAD_DIM), vs.reshape(1, bd, 1, N_KV_HEADS, HEAD_DIM),
            kin.reshape(1, bd, 1, IDX_DIM),
            jnp.concatenate([state_pool[0][:, 1:], us[:, None, :]], axis=1)[None])
```

```python
import functools
import math

import jax
import jax.numpy as jnp
import numpy as np
from jax import lax
from jax.experimental import pallas as pl
from jax.experimental.pallas import tpu as pltpu

D_MODEL = 1024
PAGE_SIZE = 128
POOL_WIDTH = D_MODEL // 2
POOL_WINDOWS = (2, 4, 8, 16)
POOL_GROUP_DIM = POOL_WIDTH // len(POOL_WINDOWS)
POOL_BUF = 15
N_HEADS = 8
N_KV_HEADS = 4
HEAD_DIM = 64
GROUP = N_HEADS // N_KV_HEADS
ATTN_WIDTH = N_HEADS * HEAD_DIM
KV_WIDTH = N_KV_HEADS * HEAD_DIM
IDX_HEADS = 4
IDX_DIM = 64
TOPK_MAX = 256
N_BUCKETS = 32
MAX_DISTANCE = 128
D_FF = -(-(8 * D_MODEL) // (3 * 256)) * 256
EPS = 1e-6
IN_SPLITS = (POOL_WIDTH, ATTN_WIDTH, KV_WIDTH, KV_WIDTH, IDX_HEADS * IDX_DIM, IDX_DIM, IDX_HEADS, 2 * D_MODEL)

LANES = 128
SUBLANES = 8
QBLK = 256
CHUNK = 512
INT_MIN = -(2 ** 31)
NEG = -1e30
LOG2E = math.log2(math.e)
VMEM_LIMIT = 56 * 1024 * 1024
V_AUG = HEAD_DIM + 16

N_U, N_KA, N_KK, N_G = 0, 512, 1024, 1152
N_END = N_G + 2 * D_MODEL
T_Q, T_K, T_V, T_QI, T_KI, T_WI = 0, 512, 768, 1024, 1280, 1344
T_END = T_WI + SUBLANES
QP_ROWS = ATTN_WIDTH + IDX_HEADS * LANES

F32 = jnp.float32
BF16 = jnp.bfloat16


def _dot(a, b, precision=None):
    return jnp.dot(a, b, preferred_element_type=F32, precision=precision)


def _dot_nt(a, b):
    return lax.dot_general(a, b, (((1,), (1,)), ((), ())), preferred_element_type=F32)


def _dot_tn(a, b):
    return lax.dot_general(a, b, (((0,), (0,)), ((), ())), preferred_element_type=F32)


def _rms(x, g):
    return (x * lax.rsqrt(jnp.mean(x * x, axis=-1, keepdims=True) + EPS)) * g


def _hi_lo(z):
    hi = z.astype(BF16).astype(F32)
    return hi, z - hi


def _sortable_key(score):
    bits = lax.bitcast_convert_type(score, jnp.int32)
    return jnp.where(bits < 0, -(bits & 0x7FFFFFFF), bits)


def _gated_pool(d_groups, gates, pw_ref, ps_ref, wpp_ref):
    a = [_dot(d.astype(BF16), pw_ref[g]) for g, d in enumerate(d_groups)]
    a = jnp.concatenate(a, axis=1) * ps_ref[...]
    ap = _dot(a.astype(BF16), wpp_ref[...])
    return jax.nn.sigmoid(gates[:, :D_MODEL]) * ap


def _proj_kernel(x_ref, ln1_ref, wn_ref, wt_ref, pw_ref, ps_ref, wpp_ref,
                 qp_ref, kT_ref, vT_ref, vTa_ref, kiT_ref, wiT_ref, kis_ref, ka_ref, mp_ref, ga_ref, ul_ref,
                 uext_ref, *, tm, tiles_per_seq):
    ti = pl.program_id(0) % tiles_per_seq
    hb = _rms(x_ref[...], ln1_ref[...]).astype(BF16)

    def mm(lo, hi):
        return _dot(hb, wn_ref[:, lo:hi])

    def mt(lo, hi):
        return _dot_nt(wt_ref[lo:hi, :], hb)

    def per_qblk(ref, row0, val):
        for i in range(tm // QBLK):
            ref[0, i, row0:row0 + val.shape[0], :] = val[:, i * QBLK:(i + 1) * QBLK]

    per_qblk(qp_ref, 0, (mt(T_Q, T_K) * (HEAD_DIM ** -0.5 * LOG2E)).astype(BF16))
    kT_ref[0] = mt(T_K, T_V)
    vT = mt(T_V, T_QI)
    vT_ref[0] = vT
    ones = jnp.ones((V_AUG - HEAD_DIM, tm), F32)
    vTa_ref[0] = jnp.concatenate(
        [piece for c in range(N_KV_HEADS) for piece in (vT[c * HEAD_DIM:(c + 1) * HEAD_DIM], ones)], axis=0).astype(BF16)
    hi, lo = _hi_lo(mt(T_QI, T_KI))
    per_qblk(qp_ref, ATTN_WIDTH, jnp.concatenate(
        [piece for h in range(IDX_HEADS) for piece in (hi[h * IDX_DIM:(h + 1) * IDX_DIM], lo[h * IDX_DIM:(h + 1) * IDX_DIM])],
        axis=0).astype(BF16))
    kiT_ref[0] = mt(T_KI, T_WI)
    per_qblk(wiT_ref, 0, mt(T_WI, T_END) * (IDX_HEADS ** -0.5 * IDX_DIM ** -0.5))

    hi, lo = _hi_lo(mm(N_KK, N_G))
    kis_ref[...] = jnp.concatenate([hi, lo], axis=1).astype(BF16)
    lane = lax.broadcasted_iota(jnp.int32, (1, N_KV_HEADS * LANES), 1) % LANES
    bias_cols = jnp.where((lane == HEAD_DIM) | (lane == HEAD_DIM + 1), 1.0, 0.0)
    ka_ref[...] = (mm(N_KA, N_KK) + bias_cols).astype(BF16)

    u = mm(N_U, N_KA)

    @pl.when(ti == 0)
    def _():
        uext_ref[0:16, :] = jnp.zeros((16, POOL_WIDTH), F32)

    @pl.when(ti != 0)
    def _():
        uext_ref[0:16, :] = uext_ref[tm:tm + 16, :]

    uext_ref[16:16 + tm, :] = u
    ul_ref[0] = uext_ref[tm:tm + 16, :]
    pos1 = ti * tm + lax.broadcasted_iota(jnp.int32, (tm, 1), 0) + 1
    ds = []
    for g, w in enumerate(POOL_WINDOWS):
        sl = slice(g * POOL_GROUP_DIM, (g + 1) * POOL_GROUP_DIM)
        win = uext_ref[16:16 + tm, sl]
        for d in range(1, w):
            win = win + uext_ref[16 - d:16 - d + tm, sl]
        cnt = jnp.minimum(w, pos1).astype(F32)
        ds.append(win / cnt - u[:, sl])
    gates = mm(N_G, N_END)
    mp_ref[...] = _gated_pool(ds, gates, pw_ref, ps_ref, wpp_ref)
    ga_ref[...] = jax.nn.sigmoid(gates[:, D_MODEL:])


def _sproj_kernel(x_ref, ln1_ref, wn_ref, wt_ref, pw_ref, ps_ref, wpp_ref, st_ref,
                  q_ref, k_ref, v_ref, ki_ref, qi_ref, wi_ref, mp_ref, ga_ref, u_ref, *, n_past):
    hb = _rms(x_ref[...], ln1_ref[...]).astype(BF16)

    def mm(lo, hi):
        return _dot(hb, wn_ref[:, lo:hi])

    def mn(lo, hi):
        return _dot_nt(hb, wt_ref[lo:hi, :])

    q_ref[...] = (mn(T_Q, T_K) * HEAD_DIM ** -0.5).astype(BF16)
    k_ref[...] = mn(T_K, T_V)
    v_ref[...] = mn(T_V, T_QI)
    qi_ref[...] = mn(T_QI, T_KI)
    ki_ref[...] = mn(T_KI, T_WI)
    wi_ref[...] = mn(T_WI, T_END) * IDX_HEADS ** -0.5
    u = mm(N_U, N_KA)
    u_ref[...] = u
    ds = []
    for g, w in enumerate(POOL_WINDOWS):
        sl = slice(g * POOL_GROUP_DIM, (g + 1) * POOL_GROUP_DIM)
        win = u[:, sl]
        for d in range(1, w):
            win = win + st_ref[POOL_BUF - d][:, sl]
        ds.append(win / float(min(w, n_past + 1)) - u[:, sl])
    gates = mm(N_G, N_END)
    mp_ref[...] = _gated_pool(ds, gates, pw_ref, ps_ref, wpp_ref)
    ga_ref[...] = jax.nn.sigmoid(gates[:, D_MODEL:])


def _bias_kernel(rb_ref, corr_ref, brow_ref, srow_ref):
    i = lax.broadcasted_iota(jnp.int32, (LANES, LANES), 0)
    j = lax.broadcasted_iota(jnp.int32, (LANES, LANES), 1)
    max_exact = N_BUCKETS // 2

    def bias_of(dist, h):
        d = jnp.maximum(dist, 1).astype(F32)
        large = max_exact + (jnp.log(d / max_exact) / math.log(MAX_DISTANCE / max_exact)
                             * (N_BUCKETS - max_exact)).astype(jnp.int32)
        bucket = jnp.where(dist < max_exact, dist, jnp.minimum(large, N_BUCKETS - 1))
        acc = jnp.zeros(dist.shape, F32)
        for b in range(N_BUCKETS):
            acc = jnp.where(bucket == b, rb_ref[b, h], acc)
        return acc

    row = lax.broadcasted_iota(jnp.int32, (HEAD_DIM, QBLK), 0)
    lane1 = lax.broadcasted_iota(jnp.int32, (1, LANES), 1)
    for h in range(N_HEADS):
        far = rb_ref[N_BUCKETS - 1, h]
        for t in range(2):
            corr_ref[t, h] = (bias_of(jnp.maximum(t * LANES + j - i, 0), h) - far) * LOG2E
        corr_ref[2, h] = jnp.zeros((LANES, LANES), F32)
        far_v = jnp.full((HEAD_DIM, QBLK), far, F32) * LOG2E
        hi, lo = _hi_lo(far_v)
        brow_ref[h] = jnp.where(row == 0, hi, jnp.where(row == 1, lo, 0.0))
        srow_ref[h:h + 1, :] = jnp.concatenate([bias_of(LANES - lane1, h), bias_of(jnp.zeros((1, LANES), jnp.int32), h),
                                       jnp.full((1, LANES), far, F32)], axis=1)


def _topk_mask_cols(keys_ref, madd_ref, nc, cw, k_top):
    part = 4 * SUBLANES
    width = keys_ref.shape[1]

    def count_ge(thr):
        def body(c, acc):
            kk = keys_ref[pl.ds(pl.multiple_of(c * cw, cw), cw), :]
            for i in range(cw // part):
                acc = jnp.where(kk[i * part:(i + 1) * part] >= thr, acc + 1, acc)
            return acc

        acc = lax.fori_loop(0, nc, body, jnp.zeros((part, width), jnp.int32))
        return jnp.sum(acc, axis=0, keepdims=True)

    def bit_step(it, thr):
        cand = thr + lax.shift_left(jnp.int32(1), 31 - it)
        return jnp.where(count_ge(cand) >= k_top, cand, thr)

    thr = lax.fori_loop(0, 32, bit_step, jnp.full((1, width), INT_MIN, jnp.int32))
    thr_eff = jnp.maximum(thr, INT_MIN + 1)

    def mask_body(c, acc):
        off = pl.multiple_of(c * cw, cw)
        sel = keys_ref[pl.ds(off, cw), :] >= thr_eff
        madd_ref[pl.ds(off, cw), :] = jnp.where(sel, 0.0, NEG)
        return acc + jnp.sum(jnp.where(sel, 1, 0).reshape(cw // part, part, width), axis=0)

    acc = lax.fori_loop(0, nc, mask_body, jnp.zeros((part, width), jnp.int32))
    n_sel = jnp.sum(acc, axis=0, keepdims=True)

    @pl.when(jnp.max(n_sel) > k_top)
    def _():
        room = (k_top - count_ge(thr + 1)).astype(F32)
        a = lax.broadcasted_iota(jnp.int32, (LANES, LANES), 0)
        b = lax.broadcasted_iota(jnp.int32, (LANES, LANES), 1)
        lower = jnp.where(b <= a, 1.0, 0.0).astype(BF16)

        def tie_body(c, seen):
            off = pl.multiple_of(c * LANES, LANES)
            kk = keys_ref[pl.ds(off, LANES), :]
            eq = kk == thr
            ones = jnp.where(eq, 1.0, 0.0)
            rank = _dot(lower, ones.astype(BF16)) + seen
            tied = jnp.where(rank <= room, 0.0, NEG)
            tied = jnp.where(kk > INT_MIN, tied, NEG)
            madd_ref[pl.ds(off, LANES), :] = jnp.where(eq, tied, jnp.where(kk > thr, 0.0, NEG))
            return seen + jnp.sum(ones, axis=0, keepdims=True)

        lax.fori_loop(0, nc * (cw // LANES), tie_body, jnp.zeros((1, width), F32))


def _topk_mask_rows(keys_ref, madd_ref, nc, cw, k_top):
    rows = keys_ref.shape[0]
    sub = cw // LANES

    def count_ge(thr):
        thr_b = jnp.broadcast_to(thr, (rows, LANES))

        def body(c, acc):
            off = pl.multiple_of(c * cw, LANES)
            for s in range(sub):
                kk = keys_ref[:, pl.ds(off + s * LANES, LANES)]
                acc = acc + jnp.where(kk >= thr_b, 1, 0)
            return acc

        acc = lax.fori_loop(0, nc, body, jnp.zeros((rows, LANES), jnp.int32))
        return jnp.sum(acc, axis=1, keepdims=True)

    def bit_step(it, thr):
        cand = thr + lax.shift_left(jnp.int32(1), 31 - it)
        return jnp.where(count_ge(cand) >= k_top, cand, thr)

    thr = lax.fori_loop(0, 32, bit_step, jnp.full((rows, 1), INT_MIN, jnp.int32))
    thr_b = jnp.broadcast_to(jnp.maximum(thr, INT_MIN + 1), (rows, LANES))

    def mask_body(c, acc):
        off = pl.multiple_of(c * cw, LANES)
        for s in range(sub):
            kk = keys_ref[:, pl.ds(off + s * LANES, LANES)]
            sel = kk >= thr_b
            madd_ref[:, pl.ds(off + s * LANES, LANES)] = jnp.where(sel, 0.0, NEG)
            acc = acc + jnp.where(sel, 1, 0)
        return acc

    acc = lax.fori_loop(0, nc, mask_body, jnp.zeros((rows, LANES), jnp.int32))
    n_sel = jnp.sum(acc, axis=1, keepdims=True)

    @pl.when(jnp.max(n_sel) > k_top)
    def _():
        room = (k_top - count_ge(thr + 1)).astype(F32)
        a = lax.broadcasted_iota(jnp.int32, (LANES, LANES), 0)
        b = lax.broadcasted_iota(jnp.int32, (LANES, LANES), 1)
        upper = jnp.where(a <= b, 1.0, 0.0).astype(BF16)
        eq_thr = jnp.broadcast_to(thr, (rows, LANES))

        def tie_body(c, seen):
            off = pl.multiple_of(c * LANES, LANES)
            kk = keys_ref[:, pl.ds(off, LANES)]
            eq = kk == eq_thr
            ones = jnp.where(eq, 1.0, 0.0)
            rank = _dot(ones.astype(BF16), upper) + seen
            tied = jnp.where(rank <= room, 0.0, NEG)
            tied = jnp.where(kk > INT_MIN, tied, NEG)
            madd_ref[:, pl.ds(off, LANES)] = jnp.where(eq, tied, jnp.where(kk > eq_thr, 0.0, NEG))
            return seen + jnp.sum(ones, axis=1, keepdims=True)

        lax.fori_loop(0, nc * sub, tie_body, jnp.zeros((rows, 1), F32))


def _attn_kernel(qp_ref, wiT_ref, kis_ref, ka_ref, vTa_ref, corr_ref, brow_ref, o_ref,
                 keys_ref, madd_ref, m_ref, acc_ref, s0_ref, s1_ref, cm0_ref, cm1_ref, p_ref, qs_ref, qa_ref, *, k_top):
    j = pl.program_id(1)
    t0 = j * QBLK
    cj = t0 // CHUNK
    nc = cj + 1
    qpos = t0 + lax.broadcasted_iota(jnp.int32, (1, QBLK), 1)

    for h in range(IDX_HEADS):
        rows = slice(ATTN_WIDTH + h * LANES, ATTN_WIDTH + (h + 1) * LANES)
        qs_ref[h] = jnp.concatenate([qp_ref[0, 0, rows, :]] * 2, axis=0)

    def score_body(c, carry):
        off = pl.multiple_of(c * CHUNK, CHUNK)
        sc = jnp.zeros((CHUNK, QBLK), F32)
        for h in range(IDX_HEADS):
            s = _dot(kis_ref[pl.ds(off, CHUNK), :], qs_ref[h])
            sc = sc + wiT_ref[0, 0, h:h + 1, :] * jnp.maximum(s, 0.0)
        kpos = off + lax.broadcasted_iota(jnp.int32, (CHUNK, 1), 0)
        keys_ref[pl.ds(off, CHUNK), :] = jnp.where(kpos <= qpos, _sortable_key(sc), INT_MIN)
        return carry

    lax.fori_loop(0, nc, score_body, 0)
    _topk_mask_cols(keys_ref, madd_ref, nc, CHUNK, k_top)

    for h in range(N_HEADS):
        qa_ref[h] = jnp.concatenate([qp_ref[0, 0, h * HEAD_DIM:(h + 1) * HEAD_DIM, :], brow_ref[h].astype(BF16)], axis=0)
    m_ref[...] = jnp.full(m_ref.shape, NEG, F32)
    acc_ref[...] = jnp.zeros(acc_ref.shape, F32)

    nblk = CHUNK // LANES
    nsub = QBLK // LANES
    s_bufs, cm_bufs = (s0_ref, s1_ref), (cm0_ref, cm1_ref)

    def logits(c, s_ref, cm_ref):
        off = pl.multiple_of(c * CHUNK, CHUNK)
        for h in range(N_HEADS):
            g = h // GROUP
            mx = jnp.full((SUBLANES, QBLK), NEG, F32)
            for i in range(nblk):
                rows = pl.ds(off + i * LANES, LANES)
                s = _dot(ka_ref[rows, g * LANES:(g + 1) * LANES], qa_ref[h]) + madd_ref[rows, :]
                tiles = [corr_ref[jnp.clip(j * nsub + q - (c * nblk + i), 0, 2), h] for q in range(nsub)]
                s = s + jnp.concatenate(tiles, axis=1)
                s_ref[h, i * LANES:(i + 1) * LANES, :] = s
                mx = jnp.maximum(mx, jnp.max(s.reshape(LANES // SUBLANES, SUBLANES, QBLK), axis=0))
            cm_ref[h] = jnp.max(mx, axis=0, keepdims=True)

    def accumulate(c, s_ref, cm_ref):
        off = pl.multiple_of(c * CHUNK, CHUNK)
        for h in range(N_HEADS):
            g = h // GROUP
            m_prev = m_ref[h]
            m_next = jnp.maximum(m_prev, cm_ref[h])
            for i in range(nblk):
                blk = slice(i * LANES, (i + 1) * LANES)
                p_ref[h, blk, :] = jnp.exp2(s_ref[h, blk, :] - m_next).astype(BF16)
            pv = _dot(vTa_ref[0, g * V_AUG:(g + 1) * V_AUG, pl.ds(off, CHUNK)], p_ref[h])
            acc_ref[h] = jnp.exp2(m_prev - m_next) * acc_ref[h] + pv
            m_ref[h] = m_next

    def step(c, carry):
        for par in (0, 1):
            @pl.when(c % 2 == par)
            def _(par=par):
                logits(c + 1, s_bufs[1 - par], cm_bufs[1 - par])
                accumulate(c, s_bufs[par], cm_bufs[par])
        return carry

    logits(0, s_bufs[0], cm_bufs[0])
    lax.fori_loop(0, cj, step, 0)
    for par in (0, 1):
        @pl.when(cj % 2 == par)
        def _(par=par):
            accumulate(cj, s_bufs[par], cm_bufs[par])

    for h in range(N_HEADS):
        acc = acc_ref[h]
        o_ref[0, 0, h * HEAD_DIM:(h + 1) * HEAD_DIM, :] = (acc[0:HEAD_DIM] / acc[HEAD_DIM:HEAD_DIM + 1]).astype(o_ref.dtype)


def _ffn_kernel(x_ref, mp_ref, ga_ref, aaT_ref, wap_ref, wout_ref, ln2_ref, wgu_ref, wd_ref, lnf_ref, y_ref, *, fc):
    aaT = jnp.concatenate([aaT_ref[0, i] for i in range(aaT_ref.shape[1])], axis=1)
    att = _dot_tn(aaT, wap_ref[...])
    m = mp_ref[...] + ga_ref[...] * att
    h = x_ref[...] + _dot(m.astype(BF16), wout_ref[...])
    hn = _rms(h, ln2_ref[...]).astype(BF16)
    y = h
    for c in range(D_FF // fc):
        gate = _dot(hn, wgu_ref[:, c * fc:(c + 1) * fc])
        up = _dot(hn, wgu_ref[:, D_FF + c * fc:D_FF + (c + 1) * fc])
        act = (gate * jax.nn.sigmoid(gate)) * up
        y = y + _dot(act.astype(BF16), wd_ref[c * fc:(c + 1) * fc, :])
    y_ref[...] = _rms(y, lnf_ref[...])


def _sscore_kernel(pt_ref, qi_ref, wi_ref, *refs, pages):
    page_refs, o_ref = refs[:pages], refs[pages]
    kc = jnp.concatenate([r[0, 0] for r in page_refs], axis=1)
    s = _dot(qi_ref[0], kc, precision=lax.Precision.HIGHEST) * IDX_DIM ** -0.5
    o_ref[0] = jnp.sum(wi_ref[0] * jnp.maximum(s, 0.0), axis=0, keepdims=True)


def _sselect_kernel(sc_ref, qi_ref, kin_ref, wi_ref, madd_ref, keys_ref, *, n_past, cw, k_top):
    rows = sc_ref.shape[0]
    keys_ref[:, 0:n_past] = _sortable_key(sc_ref[...])
    kin = kin_ref[...]
    wi = wi_ref[...]
    s_new = jnp.zeros((rows, 1), F32)
    for h in range(IDX_HEADS):
        s = jnp.sum(qi_ref[:, h * IDX_DIM:(h + 1) * IDX_DIM] * kin, axis=1, keepdims=True) * IDX_DIM ** -0.5
        s_new = s_new + wi[:, h:h + 1] * jnp.maximum(s, 0.0)
    lane = lax.broadcasted_iota(jnp.int32, (rows, keys_ref.shape[1] - n_past), 1)
    keys_ref[:, n_past:] = jnp.where(lane == 0, _sortable_key(s_new), INT_MIN)
    _topk_mask_rows(keys_ref, madd_ref, keys_ref.shape[1] // cw, cw, k_top)


def _sattn_kernel(pt_ref, qbd_ref, madd_ref, maddn_ref, sb_ref, sbn_ref, kn_ref, vn_ref, *refs, pages):
    k_refs, v_refs = refs[:pages], refs[pages:2 * pages]
    o_ref, m_ref, l_ref, acc_ref = refs[2 * pages:]
    c = pl.program_id(1)

    @pl.when(c == 0)
    def _():
        m_ref[...] = jnp.full(m_ref.shape, NEG, F32)
        l_ref[...] = jnp.zeros(l_ref.shape, F32)
        acc_ref[...] = jnp.zeros(acc_ref.shape, F32)

    qbd = qbd_ref[0]
    kc = jnp.concatenate([r[0, 0].reshape(KV_WIDTH, PAGE_SIZE) for r in k_refs], axis=1).astype(BF16)
    vc = jnp.concatenate([r[0, 0].reshape(KV_WIDTH, PAGE_SIZE) for r in v_refs], axis=1).astype(BF16)
    s = _dot(qbd, kc) + sb_ref[...] + madd_ref[0]
    m_prev = m_ref[...]
    m_next = jnp.maximum(m_prev, jnp.max(s, axis=1, keepdims=True))
    alpha = jnp.exp(m_prev - m_next)
    p = jnp.exp(s - m_next)
    l_ref[...] = alpha * l_ref[...] + jnp.sum(p, axis=1, keepdims=True)
    acc_ref[...] = alpha * acc_ref[...] + _dot_nt(p.astype(BF16), vc)
    m_ref[...] = m_next

    @pl.when(c == pl.num_programs(1) - 1)
    def _():
        s_new = (jnp.sum(qbd.astype(F32) * kn_ref[0], axis=1, keepdims=True)
                 + sbn_ref[:, 0:1] + maddn_ref[0][:, 0:1])
        m_prev = m_ref[...]
        m_next = jnp.maximum(m_prev, s_new)
        alpha = jnp.exp(m_prev - m_next)
        p = jnp.exp(s_new - m_next)
        l = alpha * l_ref[...] + p
        o_ref[0] = (alpha * acc_ref[...] + p * vn_ref[0]) / l


def _full(shape):
    return pl.BlockSpec(shape, lambda *_: (0,) * len(shape))


def _pack_w_in(w):
    offs = np.cumsum((0,) + IN_SPLITS)
    wu, wq, wk, wv, wqi, wki, wwi, wg = [w[:, offs[i]:offs[i + 1]] for i in range(len(IN_SPLITS))]
    pad = jnp.zeros((D_MODEL, LANES - HEAD_DIM), w.dtype)
    wka = jnp.concatenate([piece for c in range(N_KV_HEADS) for piece in (wk[:, c * HEAD_DIM:(c + 1) * HEAD_DIM], pad)], axis=1)
    w_nat = jnp.concatenate([wu, wka, wki, wki, wg], axis=1).astype(BF16)
    wwp = jnp.pad(wwi, ((0, 0), (0, SUBLANES - IDX_HEADS)))
    w_t = jnp.concatenate([wq, wk, wv, wqi, wki, wwp], axis=1).T.astype(BF16)
    return w_nat, w_t


def _proj_prompt(x2, ln1, w_nat, w_t, pw, ps, wpp, nb, seq, tm):
    t = x2.shape[0]
    tps = seq // tm
    row = lambda n: pl.BlockSpec((tm, n), lambda i: (i, 0))
    col = lambda n: pl.BlockSpec((1, n, tm), lambda i: (i // tps, 0, i % tps))
    tshape = lambda n, dt: jax.ShapeDtypeStruct((nb, n, seq), dt)
    qblk = lambda n: pl.BlockSpec((1, tm // QBLK, n, QBLK), lambda i: (i // tps, i % tps, 0, 0))
    out_shape = (
        jax.ShapeDtypeStruct((nb, seq // QBLK, QP_ROWS, QBLK), BF16),
        tshape(KV_WIDTH, F32), tshape(KV_WIDTH, F32),
        tshape(N_KV_HEADS * V_AUG, BF16),
        tshape(IDX_DIM, F32),
        jax.ShapeDtypeStruct((nb, seq // QBLK, SUBLANES, QBLK), F32),
        jax.ShapeDtypeStruct((t, 2 * LANES), BF16),
        jax.ShapeDtypeStruct((t, N_KV_HEADS * LANES), BF16),
        jax.ShapeDtypeStruct((t, D_MODEL), F32),
        jax.ShapeDtypeStruct((t, D_MODEL), F32),
        jax.ShapeDtypeStruct((nb, 16, POOL_WIDTH), F32),
    )
    out_specs = (qblk(QP_ROWS), col(KV_WIDTH), col(KV_WIDTH), col(N_KV_HEADS * V_AUG), col(IDX_DIM),
                 qblk(SUBLANES), row(2 * LANES), row(N_KV_HEADS * LANES), row(D_MODEL),
                 row(D_MODEL), pl.BlockSpec((1, 16, POOL_WIDTH), lambda i: (i // tps, 0, 0)))
    return pl.pallas_call(
        functools.partial(_proj_kernel, tm=tm, tiles_per_seq=tps),
        grid=(t // tm,),
        in_specs=[row(D_MODEL), _full((1, D_MODEL)), _full(w_nat.shape), _full(w_t.shape), _full(pw.shape),
                  _full((1, POOL_WIDTH)), _full(wpp.shape)],
        out_specs=out_specs,
        out_shape=out_shape,
        scratch_shapes=[pltpu.VMEM((tm + 16, POOL_WIDTH), F32)],
        compiler_params=pltpu.CompilerParams(dimension_semantics=("arbitrary",), vmem_limit_bytes=VMEM_LIMIT),
        name="proj_pool",
    )(x2, ln1, w_nat, w_t, pw, ps, wpp)


def _proj_sample(x2, ln1, w_nat, w_t, pw, ps, wpp, st, n_past):
    m = x2.shape[0]
    shapes = [(ATTN_WIDTH, BF16), (KV_WIDTH, F32), (KV_WIDTH, F32), (IDX_DIM, F32), (IDX_HEADS * IDX_DIM, F32),
              (SUBLANES, F32), (D_MODEL, F32), (D_MODEL, F32), (POOL_WIDTH, F32)]
    return pl.pallas_call(
        functools.partial(_sproj_kernel, n_past=n_past),
        out_shape=tuple(jax.ShapeDtypeStruct((m, n), dt) for n, dt in shapes),
        compiler_params=pltpu.CompilerParams(vmem_limit_bytes=VMEM_LIMIT),
        name="proj_pool_sample",
    )(x2, ln1, w_nat, w_t, pw, ps, wpp, st)


def _bias_tables(rel_bias):
    return pl.pallas_call(
        _bias_kernel,
        in_specs=[pl.BlockSpec(memory_space=pltpu.SMEM)],
        out_shape=(jax.ShapeDtypeStruct((3, N_HEADS, LANES, LANES), F32),
                   jax.ShapeDtypeStruct((N_HEADS, HEAD_DIM, QBLK), F32),
                   jax.ShapeDtypeStruct((N_HEADS, 3 * LANES), F32)),
        name="bias_tables",
    )(rel_bias)


def _attn_prompt(qp, wiT, kis, ka, vTa, corr, brow, nb, seq, k_top):
    nq = seq // QBLK
    qblk = lambda n: pl.BlockSpec((1, 1, n, QBLK), lambda b, j: (b, j, 0, 0))
    seqrow = lambda n: pl.BlockSpec((seq, n), lambda b, j: (b, 0))
    return pl.pallas_call(
        functools.partial(_attn_kernel, k_top=k_top),
        grid=(nb, nq),
        in_specs=[qblk(QP_ROWS), qblk(SUBLANES), seqrow(2 * LANES), seqrow(N_KV_HEADS * LANES),
                  pl.BlockSpec((1, N_KV_HEADS * V_AUG, seq), lambda b, j: (b, 0, 0)),
                  _full(corr.shape), _full(brow.shape)],
        out_specs=qblk(ATTN_WIDTH),
        out_shape=jax.ShapeDtypeStruct((nb, nq, ATTN_WIDTH, QBLK), BF16),
        scratch_shapes=[pltpu.VMEM((seq, QBLK), jnp.int32), pltpu.VMEM((seq, QBLK), F32),
                        pltpu.VMEM((N_HEADS, 1, QBLK), F32),
                        pltpu.VMEM((N_HEADS, V_AUG, QBLK), F32),
                        pltpu.VMEM((N_HEADS, CHUNK, QBLK), F32),
                        pltpu.VMEM((N_HEADS, CHUNK, QBLK), F32),
                        pltpu.VMEM((N_HEADS, 1, QBLK), F32),
                        pltpu.VMEM((N_HEADS, 1, QBLK), F32),
                        pltpu.VMEM((N_HEADS, CHUNK, QBLK), BF16),
                        pltpu.VMEM((IDX_HEADS, 4 * IDX_DIM, QBLK), BF16),
                        pltpu.VMEM((N_HEADS, LANES, QBLK), BF16)],
        compiler_params=pltpu.CompilerParams(dimension_semantics=("arbitrary", "arbitrary"),
                                             vmem_limit_bytes=VMEM_LIMIT),
        name="sparse_attn",
    )(qp, wiT, kis, ka, vTa, corr, brow)


def _merge_ffn(x2, mp, ga, aaT, wap, wout, ln2, wgu, wd, lnf, seq, tm):
    qb = aaT.shape[-1]
    t = x2.shape[0]
    tps = seq // tm
    row = lambda n: pl.BlockSpec((tm, n), lambda i: (i, 0))
    return pl.pallas_call(
        functools.partial(_ffn_kernel, fc=256),
        grid=(t // tm,),
        in_specs=[row(D_MODEL), row(D_MODEL), row(D_MODEL),
                  pl.BlockSpec((1, tm // qb, ATTN_WIDTH, qb), lambda i: (i // tps, i % tps, 0, 0)),
                  _full(wap.shape), _full(wout.shape), _full((1, D_MODEL)), _full(wgu.shape), _full(wd.shape),
                  _full((1, D_MODEL))],
        out_specs=row(D_MODEL),
        out_shape=jax.ShapeDtypeStruct((t, D_MODEL), F32),
        compiler_params=pltpu.CompilerParams(dimension_semantics=("arbitrary",), vmem_limit_bytes=VMEM_LIMIT),
        name="merge_ffn",
    )(x2, mp, ga, aaT, wap, wout, ln2, wgu, wd, lnf)


def _sample_scores(page_table, qi, wi4, cache_ikT, pages):
    bd, n_pages = page_table.shape
    page_spec = lambda i: pl.BlockSpec((1, 1, IDX_DIM, PAGE_SIZE), lambda b, c, pt: (0, pt[b, c * pages + i], 0, 0))
    grid_spec = pltpu.PrefetchScalarGridSpec(
        num_scalar_prefetch=1,
        grid=(bd, n_pages // pages),
        in_specs=[pl.BlockSpec((1, IDX_HEADS, IDX_DIM), lambda b, c, pt: (b, 0, 0)),
                  pl.BlockSpec((1, IDX_HEADS, 1), lambda b, c, pt: (b, 0, 0))] + [page_spec(i) for i in range(pages)],
        out_specs=pl.BlockSpec((1, 1, pages * PAGE_SIZE), lambda b, c, pt: (b, 0, c)),
    )
    return pl.pallas_call(
        functools.partial(_sscore_kernel, pages=pages),
        grid_spec=grid_spec,
        out_shape=jax.ShapeDtypeStruct((bd, 1, n_pages * PAGE_SIZE), F32),
        compiler_params=pltpu.CompilerParams(dimension_semantics=("arbitrary", "arbitrary")),
        name="sample_scores",
    )(page_table, qi, wi4, *([cache_ikT] * pages))


def _sample_select(scores, qi, ki_new, wi, n_past, cw, k_top):
    bd = scores.shape[0]
    width = n_past + LANES
    return pl.pallas_call(
        functools.partial(_sselect_kernel, n_past=n_past, cw=cw, k_top=k_top),
        out_shape=jax.ShapeDtypeStruct((bd, width), F32),
        scratch_shapes=[pltpu.VMEM((bd, width), jnp.int32)],
        compiler_params=pltpu.CompilerParams(vmem_limit_bytes=VMEM_LIMIT),
        name="sample_select",
    )(scores, qi, ki_new, wi)


def _sample_attn(page_table, qbd, madd, madd_new, sbias, sbias_new, k_new, v_new, cache_kT, cache_vT, pages):
    bd, n_pages = page_table.shape
    chunk = pages * PAGE_SIZE
    page_spec = lambda i: pl.BlockSpec((1, 1, N_KV_HEADS, HEAD_DIM, PAGE_SIZE),
                                       lambda b, c, pt: (0, pt[b, c * pages + i], 0, 0, 0))
    per_b = lambda shape: pl.BlockSpec((1,) + shape, lambda b, c, pt: (b, 0, 0))
    grid_spec = pltpu.PrefetchScalarGridSpec(
        num_scalar_prefetch=1,
        grid=(bd, n_pages // pages),
        in_specs=[per_b((N_HEADS, KV_WIDTH)),
                  pl.BlockSpec((1, 1, chunk), lambda b, c, pt: (b, 0, c)),
                  per_b((1, LANES)),
                  pl.BlockSpec((N_HEADS, chunk), lambda b, c, pt: (0, c)),
                  pl.BlockSpec((N_HEADS, LANES), lambda b, c, pt: (0, 0)),
                  per_b((1, KV_WIDTH)), per_b((1, KV_WIDTH))]
                 + [page_spec(i) for i in range(pages)] * 2,
        out_specs=per_b((N_HEADS, KV_WIDTH)),
        scratch_shapes=[pltpu.VMEM((N_HEADS, 1), F32), pltpu.VMEM((N_HEADS, 1), F32),
                        pltpu.VMEM((N_HEADS, KV_WIDTH), F32)],
    )
    return pl.pallas_call(
        functools.partial(_sattn_kernel, pages=pages),
        grid_spec=grid_spec,
        out_shape=jax.ShapeDtypeStruct((bd, N_HEADS, KV_WIDTH), F32),
        compiler_params=pltpu.CompilerParams(dimension_semantics=("arbitrary", "arbitrary"),
                                             vmem_limit_bytes=VMEM_LIMIT),
        name="sample_attn",
    )(page_table, qbd, madd, madd_new, sbias, sbias_new, k_new, v_new, *([cache_kT] * pages), *([cache_vT] * pages))


def _pick(n, prefs):
    for p in prefs:
        if n % p == 0:
            return p
    return n


def kernel(x_prompt, x_sample, cache_k, cache_v, cache_idx_k, state_pool, page_table, ln1, w_in, pool_w, pool_scale,
           w_pool_proj, w_attn_proj, w_out, ln2, w_gate_up, w_down, rel_bias, ln_final):
    assert w_in.shape[0] == 1, "one layer"
    nb, seq, _ = x_prompt.shape
    bd, dec_seq, _ = x_sample.shape
    assert dec_seq == 1 and seq % CHUNK == 0
    n_pages = page_table.shape[1]
    n_past = n_pages * PAGE_SIZE

    w_nat, w_t = _pack_w_in(w_in[0])
    ln1r, ln2r, lnfr = ln1[0][None], ln2[0][None], ln_final[None]
    pw = pool_w[0].astype(BF16)
    ps = pool_scale[0][None]
    wpp, wap, wout = w_pool_proj[0].astype(BF16), w_attn_proj[0].astype(BF16), w_out[0].astype(BF16)
    wgu, wd = w_gate_up[0].astype(BF16), w_down[0].astype(BF16)
    corr, brow, srow = _bias_tables(rel_bias)

    xp = x_prompt.reshape(nb * seq, D_MODEL)
    qp, kT, vT, vTa, kiT, wiT, kis, ka, mp, ga, ul = _proj_prompt(
        xp, ln1r, w_nat, w_t, pw, ps, wpp, nb, seq, _pick(seq, (512, 256, 128)))
    aaT = _attn_prompt(qp, wiT, kis, ka, vTa, corr, brow, nb, seq, min(TOPK_MAX, seq // 4))
    y_prompt = _merge_ffn(xp, mp, ga, aaT, wap, wout, ln2r, wgu, wd, lnfr, seq, _pick(seq, (512, 256)))
    y_prompt = y_prompt.reshape(nb, seq, D_MODEL)

    xs = x_sample.reshape(bd, D_MODEL)
    st = jnp.swapaxes(state_pool[0], 0, 1)
    qs, ks, vs, kin, qin, wis, mps, gas, us = _proj_sample(xs, ln1r, w_nat, w_t, pw, ps, wpp, st, n_past)
    pages = _pick(n_pages, (32, 16, 8, 4, 2))
    scores = _sample_scores(page_table, qin.reshape(bd, IDX_HEADS, IDX_DIM), wis[:, :IDX_HEADS, None],
                            jnp.swapaxes(cache_idx_k, 2, 3), _pick(n_pages, (64, 32, 16, 8, 4, 2))).reshape(bd, n_past)
    width = n_past + LANES
    madd = _sample_select(scores, qin, kin, wis, n_past, _pick(width, (640, 512, 384, 256, 128)),
                          min(TOPK_MAX, (n_past + 1) // 4))
    head_kv = np.arange(N_HEADS) // GROUP
    onehot = jnp.asarray(head_kv[:, None] == np.arange(N_KV_HEADS)[None, :])
    qbd = jnp.where(onehot[None, :, :, None], qs.reshape(bd, N_HEADS, 1, HEAD_DIM), 0).reshape(bd, N_HEADS, KV_WIDTH)
    far = jnp.broadcast_to(srow[:, 2 * LANES:2 * LANES + 1], (N_HEADS, n_past - LANES))
    sbias = jnp.concatenate([far, srow[:, :LANES]], axis=1)
    ao = _sample_attn(page_table, qbd, madd[:, None, :n_past], madd[:, None, n_past:], sbias, srow[:, LANES:2 * LANES],
                      ks[:, None, :], vs[:, None, :], jnp.transpose(cache_k, (0, 1, 3, 4, 2)),
                      jnp.transpose(cache_v, (0, 1, 3, 4, 2)), pages)
    aas = jnp.sum(jnp.where(onehot[None, :, :, None], ao.reshape(bd, N_HEADS, N_KV_HEADS, HEAD_DIM), 0), axis=2)
    aasT = aas.reshape(1, bd, ATTN_WIDTH).astype(BF16).transpose(0, 2, 1)[None]
    y_sample = _merge_ffn(xs, mps, gas, aasT, wap, wout, ln2r, wgu, wd, lnfr, bd, bd).reshape(bd, 1, D_MODEL)

    to_heads = lambda a: jnp.transpose(a.reshape(nb, N_KV_HEADS, HEAD_DIM, seq), (0, 3, 1, 2))[None]
    return (y_prompt, y_sample, to_heads(kT), to_heads(vT), jnp.swapaxes(kiT, 1, 2)[None], ul[None, :, 1:, :],
            ks.reshape(1, bd, 1, N_KV_HEADS, HEAD_DIM), vs.reshape(1, bd, 1, N_KV_HEADS, HEAD_DIM),
            kin.reshape(1, bd, 1, IDX_DIM),
            jnp.concatenate([state_pool[0][:, 1:], us[:, None, :]], axis=1)[None])
```

```python
import functools
import math

import jax
import jax.numpy as jnp
import numpy as np
from jax import lax
from jax.experimental import pallas as pl
from jax.experimental.pallas import tpu as pltpu

D_MODEL = 1024
PAGE_SIZE = 128
POOL_WIDTH = D_MODEL // 2
POOL_WINDOWS = (2, 4, 8, 16)
POOL_GROUP_DIM = POOL_WIDTH // len(POOL_WINDOWS)
POOL_BUF = 15
N_HEADS = 8
N_KV_HEADS = 4
HEAD_DIM = 64
GROUP = N_HEADS // N_KV_HEADS
ATTN_WIDTH = N_HEADS * HEAD_DIM
KV_WIDTH = N_KV_HEADS * HEAD_DIM
IDX_HEADS = 4
IDX_DIM = 64
TOPK_MAX = 256
N_BUCKETS = 32
MAX_DISTANCE = 128
D_FF = -(-(8 * D_MODEL) // (3 * 256)) * 256
EPS = 1e-6
IN_SPLITS = (POOL_WIDTH, ATTN_WIDTH, KV_WIDTH, KV_WIDTH, IDX_HEADS * IDX_DIM, IDX_DIM, IDX_HEADS, 2 * D_MODEL)

LANES = 128
SUBLANES = 8
QBLK = 256
CHUNK = 512
INT_MIN = -(2 ** 31)
NEG = -1e30
LOG2E = math.log2(math.e)
VMEM_LIMIT = 56 * 1024 * 1024
BF16_ROWS = 16
V_AUG = HEAD_DIM + BF16_ROWS
POOL_CARRY = -(-POOL_BUF // SUBLANES) * SUBLANES
FFN_COLS = 256

N_U, N_KA, N_KK, N_G = 0, 512, 1024, 1152
N_END = N_G + 2 * D_MODEL
T_Q, T_K, T_V, T_QI, T_KI, T_WI = 0, 512, 768, 1024, 1280, 1344
T_END = T_WI + SUBLANES
QP_ROWS = ATTN_WIDTH + IDX_HEADS * LANES

F32 = jnp.float32
BF16 = jnp.bfloat16


def _dot(a, b, precision=None):
    return jnp.dot(a, b, preferred_element_type=F32, precision=precision)


def _dot_nt(a, b):
    return lax.dot_general(a, b, (((1,), (1,)), ((), ())), preferred_element_type=F32)


def _dot_tn(a, b):
    return lax.dot_general(a, b, (((0,), (0,)), ((), ())), preferred_element_type=F32)


def _rms(x, g):
    return (x * lax.rsqrt(jnp.mean(x * x, axis=-1, keepdims=True) + EPS)) * g


def _hi_lo(z):
    hi = z.astype(BF16).astype(F32)
    return hi, z - hi


def _sortable_key(score):
    bits = lax.bitcast_convert_type(score, jnp.int32)
    return jnp.where(bits < 0, -(bits & 0x7FFFFFFF), bits)


def _gated_pool(d_groups, gates, pw_ref, ps_ref, wpp_ref):
    a = [_dot(d.astype(BF16), pw_ref[g]) for g, d in enumerate(d_groups)]
    a = jnp.concatenate(a, axis=1) * ps_ref[...]
    ap = _dot(a.astype(BF16), wpp_ref[...])
    return jax.nn.sigmoid(gates[:, :D_MODEL]) * ap


def _proj_kernel(x_ref, ln1_ref, wn_ref, wt_ref, pw_ref, ps_ref, wpp_ref,
                 qp_ref, kT_ref, vT_ref, vTa_ref, kiT_ref, wiT_ref, kis_ref, ka_ref, mp_ref, ga_ref, ul_ref,
                 uext_ref, *, tm, tiles_per_seq):
    ti = pl.program_id(0) % tiles_per_seq
    hb = _rms(x_ref[...], ln1_ref[...]).astype(BF16)

    def mm(lo, hi):
        return _dot(hb, wn_ref[:, lo:hi])

    def mt(lo, hi):
        return _dot_nt(wt_ref[lo:hi, :], hb)

    def per_qblk(ref, row0, val):
        for i in range(tm // QBLK):
            ref[0, i, row0:row0 + val.shape[0], :] = val[:, i * QBLK:(i + 1) * QBLK]

    per_qblk(qp_ref, 0, (mt(T_Q, T_K) * (HEAD_DIM ** -0.5 * LOG2E)).astype(BF16))
    kT_ref[0] = mt(T_K, T_V)
    vT = mt(T_V, T_QI)
    vT_ref[0] = vT
    ones = jnp.ones((V_AUG - HEAD_DIM, tm), F32)
    vTa_ref[0] = jnp.concatenate(
        [piece for c in range(N_KV_HEADS) for piece in (vT[c * HEAD_DIM:(c + 1) * HEAD_DIM], ones)], axis=0).astype(BF16)
    hi, lo = _hi_lo(mt(T_QI, T_KI))
    per_qblk(qp_ref, ATTN_WIDTH, jnp.concatenate(
        [piece for h in range(IDX_HEADS) for piece in (hi[h * IDX_DIM:(h + 1) * IDX_DIM], lo[h * IDX_DIM:(h + 1) * IDX_DIM])],
        axis=0).astype(BF16))
    kiT_ref[0] = mt(T_KI, T_WI)
    per_qblk(wiT_ref, 0, mt(T_WI, T_END) * (IDX_HEADS ** -0.5 * IDX_DIM ** -0.5))

    hi, lo = _hi_lo(mm(N_KK, N_G))
    kis_ref[...] = jnp.concatenate([hi, lo], axis=1).astype(BF16)
    lane = lax.broadcasted_iota(jnp.int32, (1, N_KV_HEADS * LANES), 1) % LANES
    bias_cols = jnp.where((lane == HEAD_DIM) | (lane == HEAD_DIM + 1), 1.0, 0.0)
    ka_ref[...] = (mm(N_KA, N_KK) + bias_cols).astype(BF16)

    u = mm(N_U, N_KA)

    @pl.when(ti == 0)
    def _():
        uext_ref[0:POOL_CARRY, :] = jnp.zeros((POOL_CARRY, POOL_WIDTH), F32)

    @pl.when(ti != 0)
    def _():
        uext_ref[0:POOL_CARRY, :] = uext_ref[tm:tm + POOL_CARRY, :]

    uext_ref[POOL_CARRY:POOL_CARRY + tm, :] = u
    ul_ref[0] = uext_ref[tm:tm + POOL_CARRY, :]
    pos1 = ti * tm + lax.broadcasted_iota(jnp.int32, (tm, 1), 0) + 1
    ds = []
    for g, w in enumerate(POOL_WINDOWS):
        sl = slice(g * POOL_GROUP_DIM, (g + 1) * POOL_GROUP_DIM)
        win = uext_ref[POOL_CARRY:POOL_CARRY + tm, sl]
        for d in range(1, w):
            win = win + uext_ref[POOL_CARRY - d:POOL_CARRY - d + tm, sl]
        cnt = jnp.minimum(w, pos1).astype(F32)
        ds.append(win / cnt - u[:, sl])
    gates = mm(N_G, N_END)
    mp_ref[...] = _gated_pool(ds, gates, pw_ref, ps_ref, wpp_ref)
    ga_ref[...] = jax.nn.sigmoid(gates[:, D_MODEL:])


def _sproj_kernel(x_ref, ln1_ref, wn_ref, wt_ref, pw_ref, ps_ref, wpp_ref, st_ref,
                  q_ref, k_ref, v_ref, ki_ref, qi_ref, wi_ref, mp_ref, ga_ref, u_ref, *, n_past):
    hb = _rms(x_ref[...], ln1_ref[...]).astype(BF16)

    def mm(lo, hi):
        return _dot(hb, wn_ref[:, lo:hi])

    def mn(lo, hi):
        return _dot_nt(hb, wt_ref[lo:hi, :])

    q_ref[...] = (mn(T_Q, T_K) * HEAD_DIM ** -0.5).astype(BF16)
    k_ref[...] = mn(T_K, T_V)
    v_ref[...] = mn(T_V, T_QI)
    qi_ref[...] = mn(T_QI, T_KI)
    ki_ref[...] = mn(T_KI, T_WI)
    wi_ref[...] = mn(T_WI, T_END) * IDX_HEADS ** -0.5
    u = mm(N_U, N_KA)
    u_ref[...] = u
    ds = []
    for g, w in enumerate(POOL_WINDOWS):
        sl = slice(g * POOL_GROUP_DIM, (g + 1) * POOL_GROUP_DIM)
        win = u[:, sl]
        for d in range(1, w):
            win = win + st_ref[POOL_BUF - d][:, sl]
        ds.append(win / float(min(w, n_past + 1)) - u[:, sl])
    gates = mm(N_G, N_END)
    mp_ref[...] = _gated_pool(ds, gates, pw_ref, ps_ref, wpp_ref)
    ga_ref[...] = jax.nn.sigmoid(gates[:, D_MODEL:])


def _bias_kernel(rb_ref, corr_ref, brow_ref, srow_ref):
    i = lax.broadcasted_iota(jnp.int32, (LANES, LANES), 0)
    j = lax.broadcasted_iota(jnp.int32, (LANES, LANES), 1)
    max_exact = N_BUCKETS // 2

    def bias_of(dist, h):
        d = jnp.maximum(dist, 1).astype(F32)
        large = max_exact + (jnp.log(d / max_exact) / math.log(MAX_DISTANCE / max_exact)
                             * (N_BUCKETS - max_exact)).astype(jnp.int32)
        bucket = jnp.where(dist < max_exact, dist, jnp.minimum(large, N_BUCKETS - 1))
        acc = jnp.zeros(dist.shape, F32)
        for b in range(N_BUCKETS):
            acc = jnp.where(bucket == b, rb_ref[b, h], acc)
        return acc

    row = lax.broadcasted_iota(jnp.int32, (HEAD_DIM, QBLK), 0)
    lane1 = lax.broadcasted_iota(jnp.int32, (1, LANES), 1)
    for h in range(N_HEADS):
        far = rb_ref[N_BUCKETS - 1, h]
        for t in range(2):
            corr_ref[t, h] = (bias_of(jnp.maximum(t * LANES + j - i, 0), h) - far) * LOG2E
        corr_ref[2, h] = jnp.zeros((LANES, LANES), F32)
        far_v = jnp.full((HEAD_DIM, QBLK), far, F32) * LOG2E
        hi, lo = _hi_lo(far_v)
        brow_ref[h] = jnp.where(row == 0, hi, jnp.where(row == 1, lo, 0.0))
        srow_ref[h:h + 1, :] = jnp.concatenate([bias_of(LANES - lane1, h), bias_of(jnp.zeros((1, LANES), jnp.int32), h),
                                       jnp.full((1, LANES), far, F32)], axis=1)


def _topk_mask_cols(keys_ref, madd_ref, nc, cw, k_top):
    part = 4 * SUBLANES
    width = keys_ref.shape[1]

    def count_ge(thr):
        def body(c, acc):
            kk = keys_ref[pl.ds(pl.multiple_of(c * cw, cw), cw), :]
            for i in range(cw // part):
                acc = jnp.where(kk[i * part:(i + 1) * part] >= thr, acc + 1, acc)
            return acc

        acc = lax.fori_loop(0, nc, body, jnp.zeros((part, width), jnp.int32))
        return jnp.sum(acc, axis=0, keepdims=True)

    def bit_step(it, thr):
        cand = thr + lax.shift_left(jnp.int32(1), 31 - it)
        return jnp.where(count_ge(cand) >= k_top, cand, thr)

    thr = lax.fori_loop(0, 32, bit_step, jnp.full((1, width), INT_MIN, jnp.int32))
    thr_eff = jnp.maximum(thr, INT_MIN + 1)

    def mask_body(c, acc):
        off = pl.multiple_of(c * cw, cw)
        sel = keys_ref[pl.ds(off, cw), :] >= thr_eff
        madd_ref[pl.ds(off, cw), :] = jnp.where(sel, 0.0, NEG)
        return acc + jnp.sum(jnp.where(sel, 1, 0).reshape(cw // part, part, width), axis=0)

    acc = lax.fori_loop(0, nc, mask_body, jnp.zeros((part, width), jnp.int32))
    n_sel = jnp.sum(acc, axis=0, keepdims=True)

    @pl.when(jnp.max(n_sel) > k_top)
    def _():
        room = (k_top - count_ge(thr + 1)).astype(F32)
        a = lax.broadcasted_iota(jnp.int32, (LANES, LANES), 0)
        b = lax.broadcasted_iota(jnp.int32, (LANES, LANES), 1)
        lower = jnp.where(b <= a, 1.0, 0.0).astype(BF16)

        def tie_body(c, seen):
            off = pl.multiple_of(c * LANES, LANES)
            kk = keys_ref[pl.ds(off, LANES), :]
            eq = kk == thr
            ones = jnp.where(eq, 1.0, 0.0)
            rank = _dot(lower, ones.astype(BF16)) + seen
            tied = jnp.where(rank <= room, 0.0, NEG)
            tied = jnp.where(kk > INT_MIN, tied, NEG)
            madd_ref[pl.ds(off, LANES), :] = jnp.where(eq, tied, jnp.where(kk > thr, 0.0, NEG))
            return seen + jnp.sum(ones, axis=0, keepdims=True)

        lax.fori_loop(0, nc * (cw // LANES), tie_body, jnp.zeros((1, width), F32))


def _topk_mask_rows(keys_ref, madd_ref, nc, cw, k_top):
    rows = keys_ref.shape[0]
    sub = cw // LANES

    def count_ge(thr):
        thr_b = jnp.broadcast_to(thr, (rows, LANES))

        def body(c, acc):
            off = pl.multiple_of(c * cw, LANES)
            for s in range(sub):
                kk = keys_ref[:, pl.ds(off + s * LANES, LANES)]
                acc = acc + jnp.where(kk >= thr_b, 1, 0)
            return acc

        acc = lax.fori_loop(0, nc, body, jnp.zeros((rows, LANES), jnp.int32))
        return jnp.sum(acc, axis=1, keepdims=True)

    def bit_step(it, thr):
        cand = thr + lax.shift_left(jnp.int32(1), 31 - it)
        return jnp.where(count_ge(cand) >= k_top, cand, thr)

    thr = lax.fori_loop(0, 32, bit_step, jnp.full((rows, 1), INT_MIN, jnp.int32))
    thr_b = jnp.broadcast_to(jnp.maximum(thr, INT_MIN + 1), (rows, LANES))

    def mask_body(c, acc):
        off = pl.multiple_of(c * cw, LANES)
        for s in range(sub):
            kk = keys_ref[:, pl.ds(off + s * LANES, LANES)]
            sel = kk >= thr_b
            madd_ref[:, pl.ds(off + s * LANES, LANES)] = jnp.where(sel, 0.0, NEG)
            acc = acc + jnp.where(sel, 1, 0)
        return acc

    acc = lax.fori_loop(0, nc, mask_body, jnp.zeros((rows, LANES), jnp.int32))
    n_sel = jnp.sum(acc, axis=1, keepdims=True)

    @pl.when(jnp.max(n_sel) > k_top)
    def _():
        room = (k_top - count_ge(thr + 1)).astype(F32)
        a = lax.broadcasted_iota(jnp.int32, (LANES, LANES), 0)
        b = lax.broadcasted_iota(jnp.int32, (LANES, LANES), 1)
        upper = jnp.where(a <= b, 1.0, 0.0).astype(BF16)
        eq_thr = jnp.broadcast_to(thr, (rows, LANES))

        def tie_body(c, seen):
            off = pl.multiple_of(c * LANES, LANES)
            kk = keys_ref[:, pl.ds(off, LANES)]
            eq = kk == eq_thr
            ones = jnp.where(eq, 1.0, 0.0)
            rank = _dot(ones.astype(BF16), upper) + seen
            tied = jnp.where(rank <= room, 0.0, NEG)
            tied = jnp.where(kk > INT_MIN, tied, NEG)
            madd_ref[:, pl.ds(off, LANES)] = jnp.where(eq, tied, jnp.where(kk > eq_thr, 0.0, NEG))
            return seen + jnp.sum(ones, axis=1, keepdims=True)

        lax.fori_loop(0, nc * sub, tie_body, jnp.zeros((rows, 1), F32))


def _attn_kernel(qp_ref, wiT_ref, kis_ref, ka_ref, vTa_ref, corr_ref, brow_ref, o_ref,
                 keys_ref, madd_ref, m_ref, acc_ref, s0_ref, s1_ref, cm0_ref, cm1_ref, p_ref, qs_ref, qa_ref, *, k_top):
    j = pl.program_id(1)
    t0 = j * QBLK
    cj = t0 // CHUNK
    nc = cj + 1
    qpos = t0 + lax.broadcasted_iota(jnp.int32, (1, QBLK), 1)

    for h in range(IDX_HEADS):
        rows = slice(ATTN_WIDTH + h * LANES, ATTN_WIDTH + (h + 1) * LANES)
        qs_ref[h] = jnp.concatenate([qp_ref[0, 0, rows, :]] * 2, axis=0)

    def score_body(c, carry):
        off = pl.multiple_of(c * CHUNK, CHUNK)
        sc = jnp.zeros((CHUNK, QBLK), F32)
        for h in range(IDX_HEADS):
            s = _dot(kis_ref[pl.ds(off, CHUNK), :], qs_ref[h])
            sc = sc + wiT_ref[0, 0, h:h + 1, :] * jnp.maximum(s, 0.0)
        kpos = off + lax.broadcasted_iota(jnp.int32, (CHUNK, 1), 0)
        keys_ref[pl.ds(off, CHUNK), :] = jnp.where(kpos <= qpos, _sortable_key(sc), INT_MIN)
        return carry

    lax.fori_loop(0, nc, score_body, 0)
    _topk_mask_cols(keys_ref, madd_ref, nc, CHUNK, k_top)

    for h in range(N_HEADS):
        qa_ref[h] = jnp.concatenate([qp_ref[0, 0, h * HEAD_DIM:(h + 1) * HEAD_DIM, :], brow_ref[h].astype(BF16)], axis=0)
    m_ref[...] = jnp.full(m_ref.shape, NEG, F32)
    acc_ref[...] = jnp.zeros(acc_ref.shape, F32)

    nblk = CHUNK // LANES
    nsub = QBLK // LANES
    s_bufs, cm_bufs = (s0_ref, s1_ref), (cm0_ref, cm1_ref)

    def logits(c, near, s_ref, cm_ref):
        off = pl.multiple_of(c * CHUNK, CHUNK)
        for h in range(N_HEADS):
            g = h // GROUP
            mx = jnp.full((SUBLANES, QBLK), NEG, F32)
            for i in range(nblk):
                rows = pl.ds(off + i * LANES, LANES)
                s = _dot(ka_ref[rows, g * LANES:(g + 1) * LANES], qa_ref[h]) + madd_ref[rows, :]
                if near:
                    tiles = [corr_ref[jnp.clip(j * nsub + q - (c * nblk + i), 0, 2), h] for q in range(nsub)]
                    s = s + jnp.concatenate(tiles, axis=1)
                s_ref[h, i * LANES:(i + 1) * LANES, :] = s
                mx = jnp.maximum(mx, jnp.max(s.reshape(LANES // SUBLANES, SUBLANES, QBLK), axis=0))
            cm_ref[h] = jnp.max(mx, axis=0, keepdims=True)

    def accumulate(c, s_ref, cm_ref):
        off = pl.multiple_of(c * CHUNK, CHUNK)
        for h in range(N_HEADS):
            g = h // GROUP
            m_prev = m_ref[h]
            m_next = jnp.maximum(m_prev, cm_ref[h])
            for i in range(nblk):
                blk = slice(i * LANES, (i + 1) * LANES)
                p_ref[h, blk, :] = jnp.exp2(s_ref[h, blk, :] - m_next).astype(BF16)
            pv = _dot(vTa_ref[0, g * V_AUG:(g + 1) * V_AUG, pl.ds(off, CHUNK)], p_ref[h])
            acc_ref[h] = jnp.exp2(m_prev - m_next) * acc_ref[h] + pv
            m_ref[h] = m_next

    def step(c, near):
        for par in (0, 1):
            @pl.when(c % 2 == par)
            def _(par=par):
                logits(c + 1, near, s_bufs[1 - par], cm_bufs[1 - par])
                accumulate(c, s_bufs[par], cm_bufs[par])

    def far_body(c, carry):
        step(c, False)
        return carry

    def near_body(c, carry):
        step(c, True)
        return carry

    n_far = jnp.maximum(cj - 2, 0)
    logits(0, True, s_bufs[0], cm_bufs[0])
    lax.fori_loop(0, n_far, far_body, 0)
    lax.fori_loop(n_far, cj, near_body, 0)
    for par in (0, 1):
        @pl.when(cj % 2 == par)
        def _(par=par):
            accumulate(cj, s_bufs[par], cm_bufs[par])

    for h in range(N_HEADS):
        acc = acc_ref[h]
        o_ref[0, 0, h * HEAD_DIM:(h + 1) * HEAD_DIM, :] = (acc[0:HEAD_DIM] / acc[HEAD_DIM:HEAD_DIM + 1]).astype(o_ref.dtype)


def _ffn_kernel(x_ref, mp_ref, ga_ref, aaT_ref, wap_ref, wout_ref, ln2_ref, wgu_ref, wd_ref, lnf_ref, y_ref, *, fc):
    aaT = jnp.concatenate([aaT_ref[0, i] for i in range(aaT_ref.shape[1])], axis=1)
    att = _dot_tn(aaT, wap_ref[...])
    m = mp_ref[...] + ga_ref[...] * att
    h = x_ref[...] + _dot(m.astype(BF16), wout_ref[...])
    hn = _rms(h, ln2_ref[...]).astype(BF16)
    y = h
    for c in range(D_FF // fc):
        gate = _dot(hn, wgu_ref[:, c * fc:(c + 1) * fc])
        up = _dot(hn, wgu_ref[:, D_FF + c * fc:D_FF + (c + 1) * fc])
        act = (gate * jax.nn.sigmoid(gate)) * up
        y = y + _dot(act.astype(BF16), wd_ref[c * fc:(c + 1) * fc, :])
    y_ref[...] = _rms(y, lnf_ref[...])


def _sscore_kernel(pt_ref, qi_ref, wi_ref, *refs, pages):
    page_refs, o_ref = refs[:pages], refs[pages]
    kc = jnp.concatenate([r[0, 0] for r in page_refs], axis=1)
    k_hi, k_lo = (t.astype(BF16) for t in _hi_lo(kc))
    q_hi, q_lo = (t.astype(BF16) for t in _hi_lo(qi_ref[0]))
    s = (_dot(q_hi, k_hi) + _dot(q_lo, k_hi) + _dot(q_hi, k_lo) + _dot(q_lo, k_lo)) * IDX_DIM ** -0.5
    o_ref[0] = jnp.sum(wi_ref[0] * jnp.maximum(s, 0.0), axis=0, keepdims=True)


def _sselect_kernel(sc_ref, qi_ref, kin_ref, wi_ref, madd_ref, keys_ref, *, n_past, cw, k_top):
    rows = sc_ref.shape[0]
    keys_ref[:, 0:n_past] = _sortable_key(sc_ref[...])
    kin = kin_ref[...]
    wi = wi_ref[...]
    s_new = jnp.zeros((rows, 1), F32)
    for h in range(IDX_HEADS):
        s = jnp.sum(qi_ref[:, h * IDX_DIM:(h + 1) * IDX_DIM] * kin, axis=1, keepdims=True) * IDX_DIM ** -0.5
        s_new = s_new + wi[:, h:h + 1] * jnp.maximum(s, 0.0)
    lane = lax.broadcasted_iota(jnp.int32, (rows, keys_ref.shape[1] - n_past), 1)
    keys_ref[:, n_past:] = jnp.where(lane == 0, _sortable_key(s_new), INT_MIN)
    _topk_mask_rows(keys_ref, madd_ref, keys_ref.shape[1] // cw, cw, k_top)


def _sattn_kernel(pt_ref, qbd_ref, madd_ref, maddn_ref, sb_ref, sbn_ref, kn_ref, vn_ref, *refs, pages):
    k_refs, v_refs = refs[:pages], refs[pages:2 * pages]
    o_ref, m_ref, l_ref, acc_ref = refs[2 * pages:]
    c = pl.program_id(1)

    @pl.when(c == 0)
    def _():
        m_ref[...] = jnp.full(m_ref.shape, NEG, F32)
        l_ref[...] = jnp.zeros(l_ref.shape, F32)
        acc_ref[...] = jnp.zeros(acc_ref.shape, F32)

    qbd = qbd_ref[0]
    kc = jnp.concatenate([r[0, 0].reshape(KV_WIDTH, PAGE_SIZE) for r in k_refs], axis=1).astype(BF16)
    vc = jnp.concatenate([r[0, 0].reshape(KV_WIDTH, PAGE_SIZE) for r in v_refs], axis=1).astype(BF16)
    s = _dot(qbd, kc) + sb_ref[...] + madd_ref[0]
    m_prev = m_ref[...]
    m_next = jnp.maximum(m_prev, jnp.max(s, axis=1, keepdims=True))
    alpha = jnp.exp(m_prev - m_next)
    p = jnp.exp(s - m_next)
    l_ref[...] = alpha * l_ref[...] + jnp.sum(p, axis=1, keepdims=True)
    acc_ref[...] = alpha * acc_ref[...] + _dot_nt(p.astype(BF16), vc)
    m_ref[...] = m_next

    @pl.when(c == pl.num_programs(1) - 1)
    def _():
        s_new = (jnp.sum(qbd.astype(F32) * kn_ref[0], axis=1, keepdims=True)
                 + sbn_ref[:, 0:1] + maddn_ref[0][:, 0:1])
        m_prev = m_ref[...]
        m_next = jnp.maximum(m_prev, s_new)
        alpha = jnp.exp(m_prev - m_next)
        p = jnp.exp(s_new - m_next)
        l = alpha * l_ref[...] + p
        o_ref[0] = (alpha * acc_ref[...] + p * vn_ref[0]) / l


def _full(shape):
    return pl.BlockSpec(shape, lambda *_: (0,) * len(shape))


def _pack_w_in(w):
    offs = np.cumsum((0,) + IN_SPLITS)
    wu, wq, wk, wv, wqi, wki, wwi, wg = [w[:, offs[i]:offs[i + 1]] for i in range(len(IN_SPLITS))]
    pad = jnp.zeros((D_MODEL, LANES - HEAD_DIM), w.dtype)
    wka = jnp.concatenate([piece for c in range(N_KV_HEADS) for piece in (wk[:, c * HEAD_DIM:(c + 1) * HEAD_DIM], pad)], axis=1)
    w_nat = jnp.concatenate([wu, wka, wki, wki, wg], axis=1).astype(BF16)
    wwp = jnp.pad(wwi, ((0, 0), (0, SUBLANES - IDX_HEADS)))
    w_t = jnp.concatenate([wq, wk, wv, wqi, wki, wwp], axis=1).T.astype(BF16)
    return w_nat, w_t


def _proj_prompt(x2, ln1, w_nat, w_t, pw, ps, wpp, nb, seq, tm):
    t = x2.shape[0]
    tps = seq // tm
    row = lambda n: pl.BlockSpec((tm, n), lambda i: (i, 0))
    col = lambda n: pl.BlockSpec((1, n, tm), lambda i: (i // tps, 0, i % tps))
    tshape = lambda n, dt: jax.ShapeDtypeStruct((nb, n, seq), dt)
    qblk = lambda n: pl.BlockSpec((1, tm // QBLK, n, QBLK), lambda i: (i // tps, i % tps, 0, 0))
    out_shape = (
        jax.ShapeDtypeStruct((nb, seq // QBLK, QP_ROWS, QBLK), BF16),
        tshape(KV_WIDTH, F32), tshape(KV_WIDTH, F32),
        tshape(N_KV_HEADS * V_AUG, BF16),
        tshape(IDX_DIM, F32),
        jax.ShapeDtypeStruct((nb, seq // QBLK, SUBLANES, QBLK), F32),
        jax.ShapeDtypeStruct((t, 2 * LANES), BF16),
        jax.ShapeDtypeStruct((t, N_KV_HEADS * LANES), BF16),
        jax.ShapeDtypeStruct((t, D_MODEL), F32),
        jax.ShapeDtypeStruct((t, D_MODEL), F32),
        jax.ShapeDtypeStruct((nb, POOL_CARRY, POOL_WIDTH), F32),
    )
    out_specs = (qblk(QP_ROWS), col(KV_WIDTH), col(KV_WIDTH), col(N_KV_HEADS * V_AUG), col(IDX_DIM),
                 qblk(SUBLANES), row(2 * LANES), row(N_KV_HEADS * LANES), row(D_MODEL),
                 row(D_MODEL), pl.BlockSpec((1, POOL_CARRY, POOL_WIDTH), lambda i: (i // tps, 0, 0)))
    return pl.pallas_call(
        functools.partial(_proj_kernel, tm=tm, tiles_per_seq=tps),
        grid=(t // tm,),
        in_specs=[row(D_MODEL), _full((1, D_MODEL)), _full(w_nat.shape), _full(w_t.shape), _full(pw.shape),
                  _full((1, POOL_WIDTH)), _full(wpp.shape)],
        out_specs=out_specs,
        out_shape=out_shape,
        scratch_shapes=[pltpu.VMEM((tm + POOL_CARRY, POOL_WIDTH), F32)],
        compiler_params=pltpu.CompilerParams(dimension_semantics=("arbitrary",), vmem_limit_bytes=VMEM_LIMIT),
        name="proj_pool",
    )(x2, ln1, w_nat, w_t, pw, ps, wpp)


def _proj_sample(x2, ln1, w_nat, w_t, pw, ps, wpp, st, n_past):
    m = x2.shape[0]
    shapes = [(ATTN_WIDTH, BF16), (KV_WIDTH, F32), (KV_WIDTH, F32), (IDX_DIM, F32), (IDX_HEADS * IDX_DIM, F32),
              (SUBLANES, F32), (D_MODEL, F32), (D_MODEL, F32), (POOL_WIDTH, F32)]
    return pl.pallas_call(
        functools.partial(_sproj_kernel, n_past=n_past),
        out_shape=tuple(jax.ShapeDtypeStruct((m, n), dt) for n, dt in shapes),
        compiler_params=pltpu.CompilerParams(vmem_limit_bytes=VMEM_LIMIT),
        name="proj_pool_sample",
    )(x2, ln1, w_nat, w_t, pw, ps, wpp, st)


def _bias_tables(rel_bias):
    return pl.pallas_call(
        _bias_kernel,
        in_specs=[pl.BlockSpec(memory_space=pltpu.SMEM)],
        out_shape=(jax.ShapeDtypeStruct((3, N_HEADS, LANES, LANES), F32),
                   jax.ShapeDtypeStruct((N_HEADS, HEAD_DIM, QBLK), F32),
                   jax.ShapeDtypeStruct((N_HEADS, 3 * LANES), F32)),
        name="bias_tables",
    )(rel_bias)


def _attn_prompt(qp, wiT, kis, ka, vTa, corr, brow, nb, seq, k_top):
    nq = seq // QBLK
    qblk = lambda n: pl.BlockSpec((1, 1, n, QBLK), lambda b, j: (b, j, 0, 0))
    seqrow = lambda n: pl.BlockSpec((seq, n), lambda b, j: (b, 0))
    return pl.pallas_call(
        functools.partial(_attn_kernel, k_top=k_top),
        grid=(nb, nq),
        in_specs=[qblk(QP_ROWS), qblk(SUBLANES), seqrow(2 * LANES), seqrow(N_KV_HEADS * LANES),
                  pl.BlockSpec((1, N_KV_HEADS * V_AUG, seq), lambda b, j: (b, 0, 0)),
                  _full(corr.shape), _full(brow.shape)],
        out_specs=qblk(ATTN_WIDTH),
        out_shape=jax.ShapeDtypeStruct((nb, nq, ATTN_WIDTH, QBLK), BF16),
        scratch_shapes=[pltpu.VMEM((seq, QBLK), jnp.int32), pltpu.VMEM((seq, QBLK), F32),
                        pltpu.VMEM((N_HEADS, 1, QBLK), F32),
                        pltpu.VMEM((N_HEADS, V_AUG, QBLK), F32),
                        pltpu.VMEM((N_HEADS, CHUNK, QBLK), F32),
                        pltpu.VMEM((N_HEADS, CHUNK, QBLK), F32),
                        pltpu.VMEM((N_HEADS, 1, QBLK), F32),
                        pltpu.VMEM((N_HEADS, 1, QBLK), F32),
                        pltpu.VMEM((N_HEADS, CHUNK, QBLK), BF16),
                        pltpu.VMEM((IDX_HEADS, 4 * IDX_DIM, QBLK), BF16),
                        pltpu.VMEM((N_HEADS, LANES, QBLK), BF16)],
        compiler_params=pltpu.CompilerParams(dimension_semantics=("arbitrary", "arbitrary"),
                                             vmem_limit_bytes=VMEM_LIMIT),
        name="sparse_attn",
    )(qp, wiT, kis, ka, vTa, corr, brow)


def _merge_ffn(x2, mp, ga, aaT, wap, wout, ln2, wgu, wd, lnf, seq, tm):
    qb = aaT.shape[-1]
    t = x2.shape[0]
    tps = seq // tm
    row = lambda n: pl.BlockSpec((tm, n), lambda i: (i, 0))
    return pl.pallas_call(
        functools.partial(_ffn_kernel, fc=FFN_COLS),
        grid=(t // tm,),
        in_specs=[row(D_MODEL), row(D_MODEL), row(D_MODEL),
                  pl.BlockSpec((1, tm // qb, ATTN_WIDTH, qb), lambda i: (i // tps, i % tps, 0, 0)),
                  _full(wap.shape), _full(wout.shape), _full((1, D_MODEL)), _full(wgu.shape), _full(wd.shape),
                  _full((1, D_MODEL))],
        out_specs=row(D_MODEL),
        out_shape=jax.ShapeDtypeStruct((t, D_MODEL), F32),
        compiler_params=pltpu.CompilerParams(dimension_semantics=("arbitrary",), vmem_limit_bytes=VMEM_LIMIT),
        name="merge_ffn",
    )(x2, mp, ga, aaT, wap, wout, ln2, wgu, wd, lnf)


def _sample_scores(page_table, qi, wi4, cache_ikT, pages):
    bd, n_pages = page_table.shape
    page_spec = lambda i: pl.BlockSpec((1, 1, IDX_DIM, PAGE_SIZE), lambda b, c, pt: (0, pt[b, c * pages + i], 0, 0))
    grid_spec = pltpu.PrefetchScalarGridSpec(
        num_scalar_prefetch=1,
        grid=(bd, n_pages // pages),
        in_specs=[pl.BlockSpec((1, IDX_HEADS, IDX_DIM), lambda b, c, pt: (b, 0, 0)),
                  pl.BlockSpec((1, IDX_HEADS, 1), lambda b, c, pt: (b, 0, 0))] + [page_spec(i) for i in range(pages)],
        out_specs=pl.BlockSpec((1, 1, pages * PAGE_SIZE), lambda b, c, pt: (b, 0, c)),
    )
    return pl.pallas_call(
        functools.partial(_sscore_kernel, pages=pages),
        grid_spec=grid_spec,
        out_shape=jax.ShapeDtypeStruct((bd, 1, n_pages * PAGE_SIZE), F32),
        compiler_params=pltpu.CompilerParams(dimension_semantics=("arbitrary", "arbitrary")),
        name="sample_scores",
    )(page_table, qi, wi4, *([cache_ikT] * pages))


def _sample_select(scores, qi, ki_new, wi, n_past, cw, k_top):
    bd = scores.shape[0]
    width = n_past + LANES
    return pl.pallas_call(
        functools.partial(_sselect_kernel, n_past=n_past, cw=cw, k_top=k_top),
        out_shape=jax.ShapeDtypeStruct((bd, width), F32),
        scratch_shapes=[pltpu.VMEM((bd, width), jnp.int32)],
        compiler_params=pltpu.CompilerParams(vmem_limit_bytes=VMEM_LIMIT),
        name="sample_select",
    )(scores, qi, ki_new, wi)


def _sample_attn(page_table, qbd, madd, madd_new, sbias, sbias_new, k_new, v_new, cache_kT, cache_vT, pages):
    bd, n_pages = page_table.shape
    chunk = pages * PAGE_SIZE
    page_spec = lambda i: pl.BlockSpec((1, 1, N_KV_HEADS, HEAD_DIM, PAGE_SIZE),
                                       lambda b, c, pt: (0, pt[b, c * pages + i], 0, 0, 0))
    per_b = lambda shape: pl.BlockSpec((1,) + shape, lambda b, c, pt: (b, 0, 0))
    grid_spec = pltpu.PrefetchScalarGridSpec(
        num_scalar_prefetch=1,
        grid=(bd, n_pages // pages),
        in_specs=[per_b((N_HEADS, KV_WIDTH)),
                  pl.BlockSpec((1, 1, chunk), lambda b, c, pt: (b, 0, c)),
                  per_b((1, LANES)),
                  pl.BlockSpec((N_HEADS, chunk), lambda b, c, pt: (0, c)),
                  pl.BlockSpec((N_HEADS, LANES), lambda b, c, pt: (0, 0)),
                  per_b((1, KV_WIDTH)), per_b((1, KV_WIDTH))]
                 + [page_spec(i) for i in range(pages)] * 2,
        out_specs=per_b((N_HEADS, KV_WIDTH)),
        scratch_shapes=[pltpu.VMEM((N_HEADS, 1), F32), pltpu.VMEM((N_HEADS, 1), F32),
                        pltpu.VMEM((N_HEADS, KV_WIDTH), F32)],
    )
    return pl.pallas_call(
        functools.partial(_sattn_kernel, pages=pages),
        grid_spec=grid_spec,
        out_shape=jax.ShapeDtypeStruct((bd, N_HEADS, KV_WIDTH), F32),
        compiler_params=pltpu.CompilerParams(dimension_semantics=("arbitrary", "arbitrary"),
                                             vmem_limit_bytes=VMEM_LIMIT),
        name="sample_attn",
    )(page_table, qbd, madd, madd_new, sbias, sbias_new, k_new, v_new, *([cache_kT] * pages), *([cache_vT] * pages))


def _pick(n, prefs):
    for p in prefs:
        if n % p == 0:
            return p
    return n


def kernel(x_prompt, x_sample, cache_k, cache_v, cache_idx_k, state_pool, page_table, ln1, w_in, pool_w, pool_scale,
           w_pool_proj, w_attn_proj, w_out, ln2, w_gate_up, w_down, rel_bias, ln_final):
    assert w_in.shape[0] == 1, "one layer"
    nb, seq, _ = x_prompt.shape
    bd, dec_seq, _ = x_sample.shape
    assert dec_seq == 1 and seq % CHUNK == 0
    n_pages = page_table.shape[1]
    n_past = n_pages * PAGE_SIZE

    w_nat, w_t = _pack_w_in(w_in[0])
    ln1r, ln2r, lnfr = ln1[0][None], ln2[0][None], ln_final[None]
    pw = pool_w[0].astype(BF16)
    ps = pool_scale[0][None]
    wpp, wap, wout = w_pool_proj[0].astype(BF16), w_attn_proj[0].astype(BF16), w_out[0].astype(BF16)
    wgu, wd = w_gate_up[0].astype(BF16), w_down[0].astype(BF16)
    corr, brow, srow = _bias_tables(rel_bias)

    xp = x_prompt.reshape(nb * seq, D_MODEL)
    qp, kT, vT, vTa, kiT, wiT, kis, ka, mp, ga, ul = _proj_prompt(
        xp, ln1r, w_nat, w_t, pw, ps, wpp, nb, seq, _pick(seq, (1024, 512, 256)))
    aaT = _attn_prompt(qp, wiT, kis, ka, vTa, corr, brow, nb, seq, min(TOPK_MAX, seq // 4))
    y_prompt = _merge_ffn(xp, mp, ga, aaT, wap, wout, ln2r, wgu, wd, lnfr, seq, _pick(seq, (512, 256)))
    y_prompt = y_prompt.reshape(nb, seq, D_MODEL)

    xs = x_sample.reshape(bd, D_MODEL)
    st = jnp.swapaxes(state_pool[0], 0, 1)
    qs, ks, vs, kin, qin, wis, mps, gas, us = _proj_sample(xs, ln1r, w_nat, w_t, pw, ps, wpp, st, n_past)
    pages = _pick(n_pages, (32, 16, 8, 4, 2))
    scores = _sample_scores(page_table, qin.reshape(bd, IDX_HEADS, IDX_DIM), wis[:, :IDX_HEADS, None],
                            jnp.swapaxes(cache_idx_k, 2, 3), _pick(n_pages, (64, 32, 16, 8, 4, 2))).reshape(bd, n_past)
    width = n_past + LANES
    madd = _sample_select(scores, qin, kin, wis, n_past, _pick(width, (640, 512, 384, 256, 128)),
                          min(TOPK_MAX, (n_past + 1) // 4))
    head_kv = np.arange(N_HEADS) // GROUP
    onehot = jnp.asarray(head_kv[:, None] == np.arange(N_KV_HEADS)[None, :])
    qbd = jnp.where(onehot[None, :, :, None], qs.reshape(bd, N_HEADS, 1, HEAD_DIM), 0).reshape(bd, N_HEADS, KV_WIDTH)
    far = jnp.broadcast_to(srow[:, 2 * LANES:2 * LANES + 1], (N_HEADS, n_past - LANES))
    sbias = jnp.concatenate([far, srow[:, :LANES]], axis=1)
    ao = _sample_attn(page_table, qbd, madd[:, None, :n_past], madd[:, None, n_past:], sbias, srow[:, LANES:2 * LANES],
                      ks[:, None, :], vs[:, None, :], jnp.transpose(cache_k, (0, 1, 3, 4, 2)),
                      jnp.transpose(cache_v, (0, 1, 3, 4, 2)), pages)
    aas = jnp.sum(jnp.where(onehot[None, :, :, None], ao.reshape(bd, N_HEADS, N_KV_HEADS, HEAD_DIM), 0), axis=2)
    aasT = aas.reshape(1, bd, ATTN_WIDTH).astype(BF16).transpose(0, 2, 1)[None]
    y_sample = _merge_ffn(xs, mps, gas, aasT, wap, wout, ln2r, wgu, wd, lnfr, bd, bd).reshape(bd, 1, D_MODEL)

    to_heads = lambda a: jnp.transpose(a.reshape(nb, N_KV_HEADS, HEAD_DIM, seq), (0, 3, 1, 2))[None]
    return (y_prompt, y_sample, to_heads(kT), to_heads(vT), jnp.swapaxes(kiT, 1, 2)[None],
            ul[None, :, POOL_CARRY - POOL_BUF:, :],
            ks.reshape(1, bd, 1, N_KV_HEADS, HEAD_DIM), vs.reshape(1, bd, 1, N_KV_HEADS, HEAD_DIM),
            kin.reshape(1, bd, 1, IDX_DIM),
            jnp.concatenate([state_pool[0][:, 1:], us[:, None, :]], axis=1)[None])
```
